```python
import math
import jax, jax.numpy as jnp
from jax import lax
import numpy as np

D_MODEL = 1024
BATCH = 4
SEQ = 4096
DEPTH = 1
DEC_BATCH = 128
DEC_SEQ = 4
PAST_LEN = 2048
PAGE_SIZE = 128

N_META = 16
EPS = 1e-6
D_FF = 2816
NH_M = 4
DH_M = D_MODEL // NH_M
D_M = NH_M * DH_M
CONV_W = 4
CHUNK_M = 64
NH_D = 8
DH_D = D_MODEL // (2 * NH_D)
DK_D = 2 * DH_D
DV_D = 2 * DH_D
D_D = NH_D * DV_D
ROT_DIM = DH_D // 4
ROPE_THETA = 500000.0
Q_BLOCK = 128
IN_WIDTHS = (D_M, D_M, NH_M, NH_M, NH_D * DK_D, NH_D * DK_D, D_D, D_MODEL, D_MODEL)
N_IN = sum(IN_WIDTHS)

kernel_name = "mlstm_diffattn_gated_hybrid_step"


def rmsnorm(x, g):
    xf = x.astype(jnp.float32)
    y = xf * lax.rsqrt(jnp.mean(xf * xf, axis=-1, keepdims=True) + EPS)
    return (y * g.astype(jnp.float32)).astype(x.dtype)


def swiglu(x, w_gate, w_up, w_down):
    return (jax.nn.silu(x @ w_gate) * (x @ w_up)) @ w_down


def split_columns(p):
    outs, start = [], 0
    for width in IN_WIDTHS:
        outs.append(p[..., start:start + width])
        start += width
    return outs


def partial_rope(x, pos):
    half = ROT_DIM // 2
    inv_freq = ROPE_THETA ** (-jnp.arange(half, dtype=jnp.float32) / half)
    ang = pos.astype(jnp.float32)[:, None] * inv_freq[None, :]
    cos = jnp.cos(ang)[None, :, None, None, :]
    sin = jnp.sin(ang)[None, :, None, None, :]
    xf = x.astype(jnp.float32)
    x1, x2, rest = xf[..., :half], xf[..., half:ROT_DIM], xf[..., ROT_DIM:]
    out = jnp.concatenate([x1 * cos - x2 * sin, x2 * cos + x1 * sin, rest], axis=-1)
    return out.astype(x.dtype)


def causal_conv(u, buf, w, b):
    T = u.shape[1]
    ext = jnp.concatenate([buf.astype(u.dtype), u], axis=1)
    out = b
    for j in range(CONV_W):
        out = out + ext[:, j:j + T] * w[j]
    return out, ext[:, T:]


def mlstm_chunk(state, inp):
    C, n, m = state
    q, k, v, logi, logf = inp
    L = q.shape[2]
    b = jnp.cumsum(logf, axis=-1)
    causal = jnp.tril(jnp.ones((L, L), dtype=bool))
    d = jnp.where(causal, b[..., :, None] - b[..., None, :] + logi[..., None, :], -jnp.inf)
    inter = b + m[..., None]
    m_t = jnp.maximum(jnp.max(d, axis=-1), inter)
    w_intra = jnp.exp(d - m_t[..., None])
    w_inter = jnp.exp(inter - m_t)
    s = jnp.einsum("bhtk,bhsk->bhts", q, k) * w_intra
    num = jnp.einsum("bhts,bhsv->bhtv", s, v) + w_inter[..., None] * jnp.einsum("bhvk,bhtk->bhtv", C, q)
    den = jnp.sum(s, axis=-1) + w_inter * jnp.einsum("bhk,bhtk->bht", n, q)
    h = num / jnp.maximum(jnp.abs(den), jnp.exp(-m_t))[..., None]
    m_new = m_t[..., -1]
    w_state = jnp.exp(b[..., -1:] - b + logi - m_new[..., None])
    decay = jnp.exp(b[..., -1] + m - m_new)
    C_new = decay[..., None, None] * C + jnp.einsum("bhs,bhsv,bhsk->bhvk", w_state, v, k)
    n_new = decay[..., None] * n + jnp.einsum("bhs,bhsk->bhk", w_state, k)
    return (C_new, n_new, m_new), h


def diff_attention(q, k, v, q_pos, k_pos, lam, lambda_init, subln):
    s = jnp.einsum("bqhmd,bkhmd->bhmqk", q.astype(jnp.float32), k.astype(jnp.float32)) * (DH_D ** -0.5)
    s = jnp.where(k_pos[None, :] <= q_pos[:, None], s, -jnp.inf)
    p = jax.nn.softmax(s, axis=-1)
    a = p[:, :, 0] - lam * p[:, :, 1]
    o = jnp.einsum("bhqk,bkhv->bqhv", a, v.astype(jnp.float32))
    return rmsnorm(o, subln) * (1.0 - lambda_init)


def hybrid_layer(x, pos, past_k, past_v, past_pos, C0, n0, m0, buf0, lw, layer_idx, prompt):
    (ffn1_norm, ffn1_w_gate, ffn1_w_up, ffn1_w_down, mix_norm, w_in, conv_w, conv_b,
     w_mq, w_mk, w_mv, b_igate, b_fgate, mlstm_norm, lambda_q1, lambda_k1,
     lambda_q2, lambda_k2, subln, w_proj_a, w_proj_b, w_out,
     ffn2_norm, ffn2_w_gate, ffn2_w_up, ffn2_w_down) = lw
    f32 = jnp.float32
    B, T, _ = x.shape
    lambda_init = 0.8 - 0.6 * math.exp(-0.3 * layer_idx)

    x = x + 0.5 * swiglu(rmsnorm(x, ffn1_norm), ffn1_w_gate, ffn1_w_up, ffn1_w_down)

    h = rmsnorm(x, mix_norm)
    xm, om, i_pre, f_pre, qd, kd, vd, g_a, g_b = split_columns(h @ w_in)

    c, conv_new = causal_conv(xm, buf0, conv_w, conv_b)
    c = jax.nn.silu(c).reshape(B, T, NH_M, DH_M)
    xh = xm.reshape(B, T, NH_M, DH_M)
    qm = jnp.einsum("bthd,hde->bhte", c, w_mq).astype(f32)
    km = jnp.einsum("bthd,hde->bhte", c, w_mk).astype(f32) * (DH_M ** -0.5)
    vm = jnp.einsum("bthd,hde->bhte", xh, w_mv).astype(f32)
    logi = jnp.swapaxes((i_pre + b_igate).astype(f32), 1, 2)
    logf = jnp.swapaxes(jax.nn.log_sigmoid((f_pre + b_fgate).astype(f32)), 1, 2)
    state = (C0.astype(f32), n0.astype(f32), m0.astype(f32))
    seqs = (qm, km, vm, logi, logf)
    if prompt:
        state, h_lead = mlstm_chunk(state, tuple(a[:, :, :N_META] for a in seqs))
        n_chunks = (T - N_META) // CHUNK_M
        chunks = tuple(
            jnp.moveaxis(a[:, :, N_META:].reshape(a.shape[:2] + (n_chunks, CHUNK_M) + a.shape[3:]), 2, 0)
            for a in seqs)
        state, h_rest = lax.scan(mlstm_chunk, state, chunks)
        h_rest = jnp.moveaxis(h_rest, 0, 2).reshape(B, NH_M, T - N_META, DH_M)
        hm = jnp.concatenate([h_lead, h_rest], axis=2)
    else:
        state, hm = mlstm_chunk(state, seqs)
    C_new, n_new, m_new = state
    hm = rmsnorm(jnp.swapaxes(hm, 1, 2), mlstm_norm) * jax.nn.sigmoid(om.astype(f32)).reshape(B, T, NH_M, DH_M)
    y_a = hm.reshape(B, T, D_M).astype(x.dtype) @ w_proj_a

    lam = (jnp.exp(jnp.sum(lambda_q1.astype(f32) * lambda_k1.astype(f32)))
           - jnp.exp(jnp.sum(lambda_q2.astype(f32) * lambda_k2.astype(f32))) + lambda_init)
    qd = partial_rope(qd.reshape(B, T, NH_D, 2, DH_D), pos)
    kd = partial_rope(kd.reshape(B, T, NH_D, 2, DH_D), pos)
    vd = vd.reshape(B, T, NH_D, DV_D)
    if prompt:
        o_lead = diff_attention(qd[:, :N_META], kd[:, :N_META], vd[:, :N_META],
                                pos[:N_META], pos[:N_META], lam, lambda_init, subln)
        n_blocks = (T - N_META) // Q_BLOCK
        q_blocks = jnp.moveaxis(qd[:, N_META:].reshape(B, n_blocks, Q_BLOCK, NH_D, 2, DH_D), 1, 0)
        pos_blocks = pos[N_META:].reshape(n_blocks, Q_BLOCK)
        o_rest = lax.map(lambda qp: diff_attention(qp[0], kd, vd, qp[1], pos, lam, lambda_init, subln),
                         (q_blocks, pos_blocks))
        o_rest = jnp.moveaxis(o_rest, 0, 1).reshape(B, T - N_META, NH_D, DV_D)
        o = jnp.concatenate([o_lead, o_rest], axis=1)
    else:
        k_all = jnp.concatenate([past_k.reshape(B, -1, NH_D, 2, DH_D).astype(kd.dtype), kd], axis=1)
        v_all = jnp.concatenate([past_v.astype(vd.dtype), vd], axis=1)
        k_pos = jnp.concatenate([past_pos, pos])
        o = diff_attention(qd, k_all, v_all, pos, k_pos, lam, lambda_init, subln)
    y_b = o.reshape(B, T, D_D).astype(x.dtype) @ w_proj_b

    merged = jax.nn.sigmoid(g_a) * y_a + jax.nn.sigmoid(g_b) * y_b
    x = x + merged @ w_out

    x = x + 0.5 * swiglu(rmsnorm(x, ffn2_norm), ffn2_w_gate, ffn2_w_up, ffn2_w_down)
    new_state = (kd.reshape(B, T, NH_D, DK_D), vd, C_new, n_new, m_new, conv_new)
    return x, new_state


def setup_inputs(seed: int = 0) -> dict:
    key = jax.random.key(seed)
    ks = iter(jax.random.split(key, 48))
    f32 = jnp.float32

    def nrm(shape, scale):
        return jax.random.normal(next(ks), shape, f32) * scale

    def gain(shape):
        return 1.0 + nrm(shape, 0.02)

    n_pages = PAST_LEN // PAGE_SIZE
    n_used = DEC_BATCH * n_pages
    n_pool = (n_used * 5) // 4
    x_prompt = nrm((BATCH, SEQ, D_MODEL), 1.0)
    x_sample = nrm((DEC_BATCH, DEC_SEQ, D_MODEL), 1.0)
    cache_k = nrm((DEPTH, n_pool, PAGE_SIZE, NH_D, DK_D), 1.0)
    cache_v = nrm((DEPTH, n_pool, PAGE_SIZE, NH_D, DV_D), 1.0)
    page_table = jax.random.permutation(next(ks), n_pool)[:n_used].reshape(DEC_BATCH, n_pages).astype(jnp.int32)
    state_C = nrm((DEPTH, DEC_BATCH, NH_M, DH_M, DH_M), 0.1)
    state_n = nrm((DEPTH, DEC_BATCH, NH_M, DH_M), 0.1)
    state_m = nrm((DEPTH, DEC_BATCH, NH_M), 0.5)
    state_conv = nrm((DEPTH, DEC_BATCH, CONV_W - 1, D_M), 1.0)
    return {
        "x_prompt": x_prompt,
        "x_sample": x_sample,
        "cache_k": cache_k,
        "cache_v": cache_v,
        "page_table": page_table,
        "state_C": state_C,
        "state_n": state_n,
        "state_m": state_m,
        "state_conv": state_conv,
        "meta_tokens": nrm((N_META, D_MODEL), 1.0),
        "ffn1_norm": gain((DEPTH, D_MODEL)),
        "ffn1_w_gate": nrm((DEPTH, D_MODEL, D_FF), D_MODEL ** -0.5),
        "ffn1_w_up": nrm((DEPTH, D_MODEL, D_FF), D_MODEL ** -0.5),
        "ffn1_w_down": nrm((DEPTH, D_FF, D_MODEL), D_FF ** -0.5),
        "mix_norm": gain((DEPTH, D_MODEL)),
        "w_in": nrm((DEPTH, D_MODEL, N_IN), D_MODEL ** -0.5),
        "conv_w": nrm((DEPTH, CONV_W, D_M), CONV_W ** -0.5),
        "conv_b": nrm((DEPTH, D_M), 0.02),
        "w_mq": nrm((DEPTH, NH_M, DH_M, DH_M), DH_M ** -0.5),
        "w_mk": nrm((DEPTH, NH_M, DH_M, DH_M), DH_M ** -0.5),
        "w_mv": nrm((DEPTH, NH_M, DH_M, DH_M), DH_M ** -0.5),
        "b_igate": nrm((DEPTH, NH_M), 0.1),
        "b_fgate": 3.0 + nrm((DEPTH, NH_M), 0.5),
        "mlstm_norm": gain((DEPTH, NH_M, DH_M)),
        "lambda_q1": nrm((DEPTH, DH_D), 0.1),
        "lambda_k1": nrm((DEPTH, DH_D), 0.1),
        "lambda_q2": nrm((DEPTH, DH_D), 0.1),
        "lambda_k2": nrm((DEPTH, DH_D), 0.1),
        "subln": gain((DEPTH, DV_D)),
        "w_proj_a": nrm((DEPTH, D_M, D_MODEL), D_M ** -0.5),
        "w_proj_b": nrm((DEPTH, D_D, D_MODEL), D_D ** -0.5),
        "w_out": nrm((DEPTH, D_MODEL, D_MODEL), D_MODEL ** -0.5),
        "ffn2_norm": gain((DEPTH, D_MODEL)),
        "ffn2_w_gate": nrm((DEPTH, D_MODEL, D_FF), D_MODEL ** -0.5),
        "ffn2_w_up": nrm((DEPTH, D_MODEL, D_FF), D_MODEL ** -0.5),
        "ffn2_w_down": nrm((DEPTH, D_FF, D_MODEL), D_FF ** -0.5),
        "final_norm": gain((D_MODEL,)),
    }


def reference(x_prompt, x_sample, cache_k, cache_v, page_table, state_C, state_n, state_m, state_conv,
              meta_tokens, ffn1_norm, ffn1_w_gate, ffn1_w_up, ffn1_w_down, mix_norm, w_in,
              conv_w, conv_b, w_mq, w_mk, w_mv, b_igate, b_fgate, mlstm_norm,
              lambda_q1, lambda_k1, lambda_q2, lambda_k2, subln, w_proj_a, w_proj_b, w_out,
              ffn2_norm, ffn2_w_gate, ffn2_w_up, ffn2_w_down, final_norm):
    bp = x_prompt.shape[0]
    bs, n_pages = page_table.shape
    past_len = n_pages * cache_k.shape[2]
    meta = jnp.broadcast_to(meta_tokens.astype(x_prompt.dtype)[None], (bp, N_META, D_MODEL))
    xp = jnp.concatenate([meta, x_prompt], axis=1)
    pos_p = jnp.arange(xp.shape[1], dtype=jnp.int32)
    pos_s = past_len + jnp.arange(x_sample.shape[1], dtype=jnp.int32)
    past_pos = jnp.arange(past_len, dtype=jnp.int32)
    C0p = jnp.zeros((bp, NH_M, DH_M, DH_M), jnp.float32)
    n0p = jnp.zeros((bp, NH_M, DH_M), jnp.float32)
    m0p = jnp.zeros((bp, NH_M), jnp.float32)
    buf0p = jnp.zeros((bp, CONV_W - 1, D_M), x_prompt.dtype)
    xs = x_sample
    st_p, st_s = [], []
    for l in range(DEPTH):
        lw = (ffn1_norm[l], ffn1_w_gate[l], ffn1_w_up[l], ffn1_w_down[l], mix_norm[l], w_in[l],
              conv_w[l], conv_b[l], w_mq[l], w_mk[l], w_mv[l], b_igate[l], b_fgate[l], mlstm_norm[l],
              lambda_q1[l], lambda_k1[l], lambda_q2[l], lambda_k2[l], subln[l],
              w_proj_a[l], w_proj_b[l], w_out[l],
              ffn2_norm[l], ffn2_w_gate[l], ffn2_w_up[l], ffn2_w_down[l])
        xp, sp = hybrid_layer(xp, pos_p, None, None, None, C0p, n0p, m0p, buf0p, lw, l, True)
        past_k = cache_k[l][page_table].reshape(bs, past_len, NH_D, DK_D)
        past_v = cache_v[l][page_table].reshape(bs, past_len, NH_D, DV_D)
        xs, ss = hybrid_layer(xs, pos_s, past_k, past_v, past_pos, state_C[l], state_n[l], state_m[l],
                              state_conv[l], lw, l, False)
        st_p.append(sp)
        st_s.append(ss)
    y_prompt = rmsnorm(xp[:, N_META:], final_norm)
    y_sample = rmsnorm(xs, final_norm)
    k_p, v_p, C_p, n_p, m_p, conv_p = [jnp.stack(a) for a in zip(*st_p)]
    k_s, v_s, C_s, n_s, m_s, conv_s = [jnp.stack(a) for a in zip(*st_s)]
    return (y_prompt, y_sample, k_p, v_p, C_p, n_p, m_p, conv_p, k_s, v_s, C_s, n_s, m_s, conv_s)
```

```python
import functools
import math

import jax
import jax.numpy as jnp
from jax import lax
from jax.experimental import pallas as pl
from jax.experimental.pallas import tpu as pltpu

F32 = jnp.float32
BF16 = jnp.bfloat16

D_MODEL = 1024
N_META = 16
EPS = 1e-6
D_FF = 2816
NH_M = 4
DH_M = 256
CONV_W = 4
NH_D = 8
DH_D = 64
DV_D = 128
ROT_DIM = 16
ROPE_THETA = 500000.0
LAMBDA_INIT = 0.8 - 0.6 * math.exp(-0.3 * 0)

LANES = 128
SUBLANES = 8
NEG_BIG = -1e30
VMEM_LIMIT = 56 * 1024 * 1024


def _cparams(sem):
    return pltpu.CompilerParams(dimension_semantics=sem, vmem_limit_bytes=VMEM_LIMIT)


def _rms(x, g):
    return x * lax.rsqrt(jnp.mean(x * x, axis=-1, keepdims=True) + EPS) * g


def _sigmoid(x):
    return 1.0 / (1.0 + jnp.exp(-x))


def _dot(a, b):
    return jnp.dot(a, b, preferred_element_type=F32)


def _dot_nt(a, b):
    return lax.dot_general(a, b, (((1,), (1,)), ((), ())), preferred_element_type=F32)


def _row_tile(n, want):
    t = min(want, n)
    while n % t:
        t //= 2
    assert t == n or t % SUBLANES == 0
    return t


def _full(shape):
    zeros = (0,) * len(shape)
    return pl.BlockSpec(shape, lambda *_: zeros)


def _ffn_body(x_ref, g_ref, wg_ref, wu_ref, wd_ref, fg_ref, o_ref, *, final_norm):
    x = x_ref[...]
    xb = _rms(x, g_ref[...]).astype(BF16)
    g = _dot(xb, wg_ref[...])
    u = _dot(xb, wu_ref[...])
    a = (g * _sigmoid(g) * u).astype(BF16)
    y = x + 0.5 * _dot(a, wd_ref[...])
    if final_norm:
        y = _rms(y, fg_ref[...])
    o_ref[...] = y


def _ffn(x, gain, wg, wu, wd, final_gain, *, final_norm, tm=512):
    n = x.shape[0]
    tm = _row_tile(n, tm)
    row = pl.BlockSpec((tm, D_MODEL), lambda i: (i, 0))
    return pl.pallas_call(
        functools.partial(_ffn_body, final_norm=final_norm),
        grid=(n // tm,),
        in_specs=[row, _full((1, D_MODEL)), _full(wg.shape), _full(wu.shape), _full(wd.shape),
                  _full((1, D_MODEL))],
        out_specs=row,
        out_shape=jax.ShapeDtypeStruct((n, D_MODEL), F32),
        compiler_params=_cparams(("parallel",)),
        name="ffn_final" if final_norm else "ffn",
    )(x, gain, wg, wu, wd, final_gain)


def _rope_slab(xs, c, s_up, s_dn):
    return xs * c + pltpu.roll(xs, LANES - ROT_DIM // 2, axis=1) * s_up + pltpu.roll(xs, ROT_DIM // 2, axis=1) * s_dn


def _inproj_body(x_ref, g_ref, wxm_ref, wom_ref, wq_ref, wk_ref, wv_ref, wga_ref, wgb_ref, wgate_ref,
                 gbias_ref, cq_ref, cu_ref, cd_ref,
                 xm_ref, om_ref, q_ref, k_ref, v_ref, ga_ref, gb_ref, gcol_ref, grow_ref):
    hb = _rms(x_ref[...], g_ref[...]).astype(BF16)
    xm_ref[...] = _dot(hb, wxm_ref[...])
    om_ref[...] = _dot(hb, wom_ref[...])
    v_ref[...] = _dot(hb, wv_ref[...])
    ga_ref[...] = _dot(hb, wga_ref[...])
    gb_ref[...] = _dot(hb, wgb_ref[...])
    c, s_up, s_dn = cq_ref[...], cu_ref[...], cd_ref[...]
    q = _dot(hb, wq_ref[...])
    k = _dot(hb, wk_ref[...])
    qscale = DH_D ** -0.5
    for h in range(NH_D):
        sl = slice(h * LANES, (h + 1) * LANES)
        q_ref[:, sl] = (_rope_slab(q[:, sl], c, s_up, s_dn) * qscale).astype(BF16)
        k_ref[:, sl] = _rope_slab(k[:, sl], c, s_up, s_dn)
    pre = _dot(hb, wgate_ref[...]) + gbias_ref[...]
    lane = lax.broadcasted_iota(jnp.int32, pre.shape, 1)
    logsig = jnp.minimum(pre, 0.0) - jnp.log(1.0 + jnp.exp(-jnp.abs(pre)))
    glog = jnp.where(lane < NH_M, pre, logsig)
    gcol_ref[...] = glog
    grow_ref[...] = glog.T[:SUBLANES, :]


def _inproj(x, gain, w, gbias, rope_tabs, *, n_rope_blocks, tm=256):
    n = x.shape[0]
    tm = _row_tile(n, tm)
    row = pl.BlockSpec((tm, D_MODEL), lambda i: (i, 0))
    tab = pl.BlockSpec((tm, LANES), lambda i: (i % n_rope_blocks, 0))
    wspec = _full((D_MODEL, D_MODEL))
    out_shapes = (
        jax.ShapeDtypeStruct((n, D_MODEL), F32),
        jax.ShapeDtypeStruct((n, D_MODEL), F32),
        jax.ShapeDtypeStruct((n, D_MODEL), BF16),
        jax.ShapeDtypeStruct((n, D_MODEL), F32),
        jax.ShapeDtypeStruct((n, D_MODEL), F32),
        jax.ShapeDtypeStruct((n, D_MODEL), F32),
        jax.ShapeDtypeStruct((n, D_MODEL), F32),
        jax.ShapeDtypeStruct((n, LANES), F32),
        jax.ShapeDtypeStruct((SUBLANES, n), F32),
    )
    return pl.pallas_call(
        _inproj_body,
        grid=(n // tm,),
        in_specs=[row, _full((1, D_MODEL))] + [wspec] * 7 + [_full((D_MODEL, LANES)), _full((1, LANES)),
                                                              tab, tab, tab],
        out_specs=(row, row, row, row, row, row, row,
                   pl.BlockSpec((tm, LANES), lambda i: (i, 0)),
                   pl.BlockSpec((SUBLANES, tm), lambda i: (0, i))),
        out_shape=out_shapes,
        compiler_params=_cparams(("parallel",)),
        name="inproj",
    )(x, gain, w["xm"], w["om"], w["q"], w["k"], w["v"], w["ga"], w["gb"], w["gate"], gbias, *rope_tabs)


def _mlstm_body(xm_ref, xprev_ref, buf0_ref, om_ref, gcol_ref, grow_ref, cw_ref, cb_ref,
                wq_ref, wk_ref, wv_ref, wvt_ref, norm_ref, c0_ref, n0_ref, m0_ref,
                hm_ref, c_ref, n_ref, m_ref, ext_ref, *, chunk):
    ci = pl.program_id(1)
    L = chunk

    @pl.when(ci == 0)
    def _():
        c_ref[...] = c0_ref[...]
        n_ref[...] = n0_ref[...]
        m_ref[...] = m0_ref[...]
        ext_ref[0:SUBLANES, :] = buf0_ref[0]

    @pl.when(ci > 0)
    def _():
        ext_ref[0:SUBLANES, :] = xprev_ref[0]

    x = xm_ref[0]
    ext_ref[SUBLANES:SUBLANES + L, :] = x
    cv = cb_ref[...]
    for j in range(CONV_W):
        cv = cv + ext_ref[pl.ds(SUBLANES - (CONV_W - 1) + j, L), :] * cw_ref[j:j + 1, :]
    cact = cv * _sigmoid(cv)

    ti = lax.broadcasted_iota(jnp.int32, (L, L), 0)
    si = lax.broadcasted_iota(jnp.int32, (L, L), 1)
    tril = si <= ti
    gcol = gcol_ref[0]
    grow = grow_ref[0, 0]
    om = om_ref[0]

    for h in range(NH_M):
        hs = slice(h * DH_M, (h + 1) * DH_M)
        chb = cact[:, hs].astype(BF16)
        xhb = x[:, hs].astype(BF16)
        q = _dot(chb, wq_ref[h])
        k = _dot(chb, wk_ref[h])
        v = _dot(xhb, wv_ref[h])
        vt = _dot_nt(wvt_ref[h], xhb)
        qb, kb, vb = q.astype(BF16), k.astype(BF16), v.astype(BF16)

        logi_c, logf_c = gcol[:, h:h + 1], gcol[:, NH_M + h:NH_M + h + 1]
        logi_r, logf_r = grow[h:h + 1, :], grow[NH_M + h:NH_M + h + 1, :]
        b_c = jnp.sum(jnp.where(tril, logf_r, 0.0), axis=1, keepdims=True)
        b_r = jnp.sum(jnp.where(ti <= si, logf_c, 0.0), axis=0, keepdims=True)
        m_prev = m_ref[0, h:h + 1, 0:1]
        n_prev = n_ref[0, h:h + 1, :]
        c_prev = c_ref[0, h]

        d = jnp.where(tril, b_c - b_r + logi_r, -jnp.inf)
        inter = b_c + m_prev
        m_t = jnp.maximum(jnp.max(d, axis=1, keepdims=True), inter)
        w_intra = jnp.exp(d - m_t)
        w_inter = jnp.exp(inter - m_t)
        s = _dot_nt(qb, kb) * w_intra
        num = _dot(s.astype(BF16), vb) + w_inter * _dot_nt(qb, c_prev.astype(BF16))
        den = jnp.sum(s, axis=1, keepdims=True) + w_inter * jnp.sum(q * n_prev, axis=1, keepdims=True)
        hh = num / jnp.maximum(jnp.abs(den), jnp.exp(-m_t))

        m_new = m_t[L - 1:L, :]
        b_last = b_c[L - 1:L, :]
        w_state_r = jnp.exp(b_last - b_r + logi_r - m_new)
        w_state_c = jnp.exp(b_last - b_c + logi_c - m_new)
        decay = jnp.exp(b_last + m_prev - m_new)
        c_ref[0, h] = decay * c_prev + _dot((vt * w_state_r).astype(BF16), kb)
        n_ref[0, h:h + 1, :] = decay * n_prev + jnp.sum(k * w_state_c, axis=0, keepdims=True)
        m_ref[0, h:h + 1, :] = jnp.broadcast_to(m_new, (1, LANES))

        hn = hh * lax.rsqrt(jnp.mean(hh * hh, axis=1, keepdims=True) + EPS) * norm_ref[:, hs]
        hm_ref[0, :, hs] = (hn * _sigmoid(om[:, hs])).astype(BF16)


def _mlstm(xm, om, gcol, grow, buf0, c0, n0, m0, w, *, chunk):
    B, T, _ = xm.shape
    L = chunk
    nc = T // L
    shared = c0.shape[0] == 1
    st = (lambda b, c: (0, 0, 0)) if shared else (lambda b, c: (b, 0, 0))
    st4 = (lambda b, c: (0, 0, 0, 0)) if shared else (lambda b, c: (b, 0, 0, 0))
    seq = pl.BlockSpec((1, L, D_MODEL), lambda b, c: (b, c, 0))
    wsq = _full((NH_M, DH_M, DH_M))
    return pl.pallas_call(
        functools.partial(_mlstm_body, chunk=L),
        grid=(B, nc),
        in_specs=[
            seq,
            pl.BlockSpec((1, SUBLANES, D_MODEL), lambda b, c: (b, jnp.maximum(c * (L // SUBLANES) - 1, 0), 0)),
            pl.BlockSpec((1, SUBLANES, D_MODEL), st),
            seq,
            pl.BlockSpec((1, L, LANES), lambda b, c: (b, c, 0)),
            pl.BlockSpec((1, 1, SUBLANES, L), lambda b, c: (b, c, 0, 0)),
            _full((CONV_W, D_MODEL)), _full((1, D_MODEL)),
            wsq, wsq, wsq, wsq, _full((1, D_MODEL)),
            pl.BlockSpec((1, NH_M, DH_M, DH_M), st4),
            pl.BlockSpec((1, SUBLANES, DH_M), st),
            pl.BlockSpec((1, SUBLANES, LANES), st),
        ],
        out_specs=(
            seq,
            pl.BlockSpec((1, NH_M, DH_M, DH_M), lambda b, c: (b, 0, 0, 0)),
            pl.BlockSpec((1, SUBLANES, DH_M), lambda b, c: (b, 0, 0)),
            pl.BlockSpec((1, SUBLANES, LANES), lambda b, c: (b, 0, 0)),
        ),
        out_shape=(
            jax.ShapeDtypeStruct((B, T, D_MODEL), BF16),
            jax.ShapeDtypeStruct((B, NH_M, DH_M, DH_M), F32),
            jax.ShapeDtypeStruct((B, SUBLANES, DH_M), F32),
            jax.ShapeDtypeStruct((B, SUBLANES, LANES), F32),
        ),
        scratch_shapes=[pltpu.VMEM((SUBLANES + L, D_MODEL), F32)],
        compiler_params=_cparams(("parallel", "arbitrary")),
        name="mlstm",
    )(xm, xm, buf0, om, gcol, grow, w["conv_w"], w["conv_b"], w["mq"], w["mk"], w["mv"], w["mvt"],
      w["mnorm"], c0, n0, m0)


def _lam(lq1_ref, lk1_ref, lq2_ref, lk2_ref):
    return (jnp.exp(jnp.sum(lq1_ref[...] * lk1_ref[...], axis=1, keepdims=True))
            - jnp.exp(jnp.sum(lq2_ref[...] * lk2_ref[...], axis=1, keepdims=True)) + LAMBDA_INIT)


def _softmax_step(state, s1, s2, vb):
    m1, l1, a1, m2, l2, a2 = state
    n1 = jnp.maximum(m1, jnp.max(s1, axis=1, keepdims=True))
    n2 = jnp.maximum(m2, jnp.max(s2, axis=1, keepdims=True))
    p1 = jnp.exp(s1 - n1)
    p2 = jnp.exp(s2 - n2)
    r1 = jnp.exp(m1 - n1)
    r2 = jnp.exp(m2 - n2)
    l1 = r1 * l1 + jnp.sum(p1, axis=1, keepdims=True)
    l2 = r2 * l2 + jnp.sum(p2, axis=1, keepdims=True)
    a1 = r1 * a1 + _dot(p1.astype(BF16), vb)
    a2 = r2 * a2 + _dot(p2.astype(BF16), vb)
    return n1, l1, a1, n2, l2, a2


def _attn_finish(state, lam, subln):
    _, l1, a1, _, l2, a2 = state
    o = a1 / l1 - lam * (a2 / l2)
    return _rms(o, subln) * (1.0 - LAMBDA_INIT)


def _attn_prompt_body(q_ref, k_ref, v_ref, kp_ref, vp_ref, lq1_ref, lk1_ref, lq2_ref, lk2_ref, subln_ref,
                      o_ref, *, tq, has_prefix):
    qi = pl.program_id(2)
    q = q_ref[0]
    lane = lax.broadcasted_iota(jnp.int32, q.shape, 1)
    q1 = jnp.where(lane < DH_D, q, jnp.zeros_like(q))
    q2 = jnp.where(lane < DH_D, jnp.zeros_like(q), q)

    def scores(kb):
        return _dot_nt(q1, kb), _dot_nt(q2, kb)

    neg = jnp.full((tq, 1), -jnp.inf, F32)
    zero = jnp.zeros((tq, 1), F32)
    zacc = jnp.zeros((tq, DV_D), F32)
    state = (neg, zero, zacc, neg, zero, zacc)

    kd = k_ref[0, pl.ds(pl.multiple_of(qi * tq, tq), tq), :].astype(BF16)
    vd = v_ref[0, pl.ds(pl.multiple_of(qi * tq, tq), tq), :].astype(BF16)
    s1, s2 = scores(kd)
    causal = lax.broadcasted_iota(jnp.int32, (tq, tq), 1) <= lax.broadcasted_iota(jnp.int32, (tq, tq), 0)
    state = _softmax_step(state, jnp.where(causal, s1, -jnp.inf), jnp.where(causal, s2, -jnp.inf), vd)

    if has_prefix:
        s1, s2 = scores(kp_ref[...].astype(BF16))
        state = _softmax_step(state, s1, s2, vp_ref[...].astype(BF16))

    def body(j, st):
        start = pl.multiple_of(j * tq, tq)
        kb = k_ref[0, pl.ds(start, tq), :].astype(BF16)
        vb = v_ref[0, pl.ds(start, tq), :].astype(BF16)
        s1, s2 = scores(kb)
        return _softmax_step(st, s1, s2, vb)

    state = lax.fori_loop(0, qi, body, state)
    lam = _lam(lq1_ref, lk1_ref, lq2_ref, lk2_ref)
    o_ref[0] = _attn_finish(state, lam, subln_ref[...]).astype(BF16)


def _attn_prompt(q, k, v, kpre, vpre, lams, subln, *, tq, has_prefix):
    B, T, _ = q.shape
    P = kpre.shape[0]
    lspec = _full((1, DH_D))
    return pl.pallas_call(
        functools.partial(_attn_prompt_body, tq=tq, has_prefix=has_prefix),
        grid=(B, NH_D, T // tq),
        in_specs=[
            pl.BlockSpec((1, tq, LANES), lambda b, h, i: (b, i, h)),
            pl.BlockSpec((1, T, LANES), lambda b, h, i: (b, 0, h)),
            pl.BlockSpec((1, T, LANES), lambda b, h, i: (b, 0, h)),
            pl.BlockSpec((P, LANES), lambda b, h, i: (0, h)),
            pl.BlockSpec((P, LANES), lambda b, h, i: (0, h)),
            lspec, lspec, lspec, lspec, _full((1, DV_D)),
        ],
        out_specs=pl.BlockSpec((1, tq, LANES), lambda b, h, i: (b, i, h)),
        out_shape=jax.ShapeDtypeStruct((B, T, D_MODEL), BF16),
        compiler_params=_cparams(("parallel", "parallel", "arbitrary")),
        name="attn_prompt",
    )(q, k, v, kpre, vpre, *lams, subln)


QROWS = 8


def _attn_sample_body(pt_ref, q_ref, kc_ref, vc_ref, kn_ref, vn_ref, lq1_ref, lk1_ref, lq2_ref, lk2_ref,
                      subln_ref, o_ref, w_ref, m_ref, l_ref, acc_ref, *, n_new, page):
    p = pl.program_id(1)
    R = 2 * NH_D * QROWS

    @pl.when(p == 0)
    def _():
        q = q_ref[0].astype(F32)
        qt = jnp.concatenate([q] * (2 * NH_D), axis=0)
        row = lax.broadcasted_iota(jnp.int32, qt.shape, 0)
        lane = lax.broadcasted_iota(jnp.int32, qt.shape, 1)
        head = (row % (NH_D * QROWS)) // QROWS
        mp = row // (NH_D * QROWS)
        w_ref[...] = jnp.where(lane // DH_D == 2 * head + mp, qt, 0.0).astype(BF16)
        m_ref[...] = jnp.full(m_ref.shape, -jnp.inf, F32)
        l_ref[...] = jnp.zeros(l_ref.shape, F32)
        acc_ref[...] = jnp.zeros(acc_ref.shape, F32)

    def step(s, vb):
        m_old = m_ref[...]
        m_new = jnp.maximum(m_old, jnp.max(s, axis=1, keepdims=True))
        pr = jnp.exp(s - m_new)
        r = jnp.exp(m_old - m_new)
        l_ref[...] = r * l_ref[...] + jnp.sum(pr, axis=1, keepdims=True)
        acc_ref[...] = r * acc_ref[...] + _dot(pr.astype(BF16), vb)
        m_ref[...] = m_new

    w = w_ref[...]
    step(_dot_nt(w, kc_ref[0].astype(BF16)), vc_ref[0].astype(BF16))

    @pl.when(p == pl.num_programs(1) - 1)
    def _():
        zpad = jnp.zeros((page - QROWS, D_MODEL), F32)
        kn = jnp.concatenate([kn_ref[0], zpad], axis=0).astype(BF16)
        vn = jnp.concatenate([vn_ref[0], zpad], axis=0).astype(BF16)
        s = _dot_nt(w, kn)
        qidx = lax.broadcasted_iota(jnp.int32, s.shape, 0) % QROWS
        kidx = lax.broadcasted_iota(jnp.int32, s.shape, 1)
        step(jnp.where((kidx <= qidx) & (kidx < n_new), s, -jnp.inf), vn)
        lam = _lam(lq1_ref, lk1_ref, lq2_ref, lk2_ref)
        half = R // 2
        o = acc_ref[0:half, :] / l_ref[0:half, :] - lam * (acc_ref[half:R, :] / l_ref[half:R, :])
        row = lax.broadcasted_iota(jnp.int32, o.shape, 0)
        lane = lax.broadcasted_iota(jnp.int32, o.shape, 1)
        o = jnp.where(lane // DV_D == row // QROWS, o, 0.0)
        o = jnp.sum(o.reshape(NH_D, QROWS, D_MODEL), axis=0)
        slabs = [_rms(o[:, h * DV_D:(h + 1) * DV_D], subln_ref[...]) for h in range(NH_D)]
        o_ref[0] = (jnp.concatenate(slabs, axis=1) * (1.0 - LAMBDA_INIT)).astype(BF16)


def _attn_sample(page_table, q, cache_k, cache_v, knew, vnew, lams, subln, *, n_new):
    B, n_pages = page_table.shape
    page = cache_k.shape[1]
    R = 2 * NH_D * QROWS
    lspec = pl.BlockSpec((1, DH_D), lambda b, p, pt: (0, 0))
    seq = pl.BlockSpec((1, QROWS, D_MODEL), lambda b, p, pt: (b, 0, 0))
    pg = pl.BlockSpec((1, page, D_MODEL), lambda b, p, pt: (pt[b, p], 0, 0))
    return pl.pallas_call(
        functools.partial(_attn_sample_body, n_new=n_new, page=page),
        grid_spec=pltpu.PrefetchScalarGridSpec(
            num_scalar_prefetch=1,
            grid=(B, n_pages),
            in_specs=[seq, pg, pg, seq, seq, lspec, lspec, lspec, lspec,
                      pl.BlockSpec((1, DV_D), lambda b, p, pt: (0, 0))],
            out_specs=seq,
            scratch_shapes=[pltpu.VMEM((R, D_MODEL), BF16), pltpu.VMEM((R, 1), F32), pltpu.VMEM((R, 1), F32),
                            pltpu.VMEM((R, D_MODEL), F32)],
        ),
        out_shape=jax.ShapeDtypeStruct((B, QROWS, D_MODEL), BF16),
        compiler_params=_cparams(("parallel", "arbitrary")),
        name="attn_sample",
    )(page_table, q, cache_k, cache_v, knew, vnew, *lams, subln)


def _merge_body(hm_ref, o_ref, ga_ref, gb_ref, x_ref, wa_ref, wb_ref, wo_ref, y_ref):
    ya = _dot(hm_ref[...], wa_ref[...])
    yb = _dot(o_ref[...], wb_ref[...])
    merged = _sigmoid(ga_ref[...]) * ya + _sigmoid(gb_ref[...]) * yb
    y_ref[...] = x_ref[...] + _dot(merged.astype(BF16), wo_ref[...])


def _merge(hm, o, ga, gb, x, wa, wb, wo, *, tm=512):
    n = x.shape[0]
    tm = _row_tile(n, tm)
    row = pl.BlockSpec((tm, D_MODEL), lambda i: (i, 0))
    wspec = _full((D_MODEL, D_MODEL))
    return pl.pallas_call(
        _merge_body,
        grid=(n // tm,),
        in_specs=[row, row, row, row, row, wspec, wspec, wspec],
        out_specs=row,
        out_shape=jax.ShapeDtypeStruct((n, D_MODEL), F32),
        compiler_params=_cparams(("parallel",)),
        name="merge",
    )(hm, o, ga, gb, x, wa, wb, wo)


def _rope_tables(pos):
    half = ROT_DIM // 2
    inv_freq = ROPE_THETA ** (-jnp.arange(half, dtype=F32) / half)
    ang = pos.astype(F32)[:, None] * inv_freq[None, :]
    cos, sin = jnp.cos(ang), jnp.sin(ang)
    n = pos.shape[0]
    ones = jnp.ones((n, DH_D - ROT_DIM), F32)
    zeros = jnp.zeros((n, DH_D - ROT_DIM), F32)
    zh = jnp.zeros((n, half), F32)
    c = jnp.concatenate([cos, cos, ones] * 2, axis=1)
    s_up = jnp.concatenate([-sin, zh, zeros] * 2, axis=1)
    s_dn = jnp.concatenate([zh, sin, zeros] * 2, axis=1)
    return c, s_up, s_dn


def _pad_rows(a, rows, axis, front=False):
    pad = [(0, 0)] * a.ndim
    pad[axis] = (rows - a.shape[axis], 0) if front else (0, rows - a.shape[axis])
    return jnp.pad(a, pad)


def kernel(x_prompt, x_sample, cache_k, cache_v, page_table, state_C, state_n, state_m, state_conv, meta_tokens, ffn1_norm, ffn1_w_gate, ffn1_w_up, ffn1_w_down, mix_norm, w_in, conv_w, conv_b, w_mq, w_mk, w_mv, b_igate, b_fgate, mlstm_norm, lambda_q1, lambda_k1, lambda_q2, lambda_k2, subln, w_proj_a, w_proj_b, w_out, ffn2_norm, ffn2_w_gate, ffn2_w_up, ffn2_w_down, final_norm):
    bp, seq, _ = x_prompt.shape
    bs, ts, _ = x_sample.shape
    n_pages = page_table.shape[1]
    page = cache_k.shape[2]
    past_len = n_pages * page
    assert w_in.shape[0] == 1 and ts <= QROWS

    win = w_in[0]
    offs = [0]
    for width in (D_MODEL, D_MODEL, NH_M, NH_M, D_MODEL, D_MODEL, D_MODEL, D_MODEL, D_MODEL):
        offs.append(offs[-1] + width)
    col = lambda i: win[:, offs[i]:offs[i + 1]]
    wgate = jnp.pad(jnp.concatenate([col(2), col(3)], axis=1), ((0, 0), (0, LANES - 2 * NH_M)))
    wproj = {"xm": col(0).astype(BF16), "om": col(1).astype(BF16), "q": col(4).astype(BF16),
             "k": col(5).astype(BF16), "v": col(6).astype(BF16), "ga": col(7).astype(BF16),
             "gb": col(8).astype(BF16), "gate": wgate.astype(BF16)}
    gbias = jnp.pad(jnp.concatenate([b_igate[0], b_fgate[0]])[None, :], ((0, 0), (0, LANES - 2 * NH_M)))
    wm = {"conv_w": conv_w[0], "conv_b": conv_b[0][None, :],
          "mq": w_mq[0].astype(BF16), "mk": (w_mk[0] * DH_M ** -0.5).astype(BF16), "mv": w_mv[0].astype(BF16),
          "mvt": jnp.swapaxes(w_mv[0], 1, 2).astype(BF16), "mnorm": mlstm_norm[0].reshape(1, D_MODEL)}
    lams = (lambda_q1, lambda_k1, lambda_q2, lambda_k2)
    row = lambda a: a.reshape(1, -1)
    f1 = (row(ffn1_norm[0]), ffn1_w_gate[0].astype(BF16), ffn1_w_up[0].astype(BF16), ffn1_w_down[0].astype(BF16))
    f2 = (row(ffn2_norm[0]), ffn2_w_gate[0].astype(BF16), ffn2_w_up[0].astype(BF16), ffn2_w_down[0].astype(BF16))
    fin = row(final_norm)
    wa, wb, wo = w_proj_a[0].astype(BF16), w_proj_b[0].astype(BF16), w_out[0].astype(BF16)

    n_main = bp * seq
    n_samp = bs * ts
    x_main = x_prompt.reshape(n_main, D_MODEL)
    x_small = jnp.concatenate([x_sample.reshape(n_samp, D_MODEL), meta_tokens], axis=0)

    tm_in = _row_tile(seq, 256)
    tabs_main = _rope_tables(N_META + jnp.arange(seq, dtype=jnp.int32))
    pos_small = jnp.concatenate([jnp.tile(past_len + jnp.arange(ts, dtype=jnp.int32), bs),
                                 jnp.arange(N_META, dtype=jnp.int32)])
    tabs_small = _rope_tables(pos_small)

    x1_main = _ffn(x_main, *f1, fin, final_norm=False)
    x1_small = _ffn(x_small, *f1, fin, final_norm=False)
    pm = _inproj(x1_main, row(mix_norm[0]), wproj, gbias, tabs_main, n_rope_blocks=seq // tm_in, tm=tm_in)
    ps = _inproj(x1_small, row(mix_norm[0]), wproj, gbias, tabs_small, n_rope_blocks=1, tm=x_small.shape[0])
    xm_m, om_m, q_m, k_m, v_m, ga_m, gb_m, gcol_m, grow_m = pm
    xm_s, om_s, q_s, k_s, v_s, ga_s, gb_s, gcol_s, grow_s = ps

    def split(a):
        return a[:n_samp], a[n_samp:]

    xm_smp, xm_meta = split(xm_s)
    om_smp, om_meta = split(om_s)
    gcol_smp, gcol_meta = split(gcol_s)
    grow_smp, grow_meta = grow_s[:, :n_samp], grow_s[:, n_samp:]

    zc = jnp.zeros((1, NH_M, DH_M, DH_M), F32)
    zn = jnp.zeros((1, SUBLANES, DH_M), F32)
    zm = jnp.zeros((1, SUBLANES, LANES), F32)
    zbuf = jnp.zeros((1, SUBLANES, D_MODEL), F32)
    hm_meta, c_meta, n_meta, m_meta = _mlstm(
        xm_meta[None], om_meta[None], gcol_meta[None], grow_meta.reshape(1, 1, SUBLANES, N_META),
        zbuf, zc, zn, zm, wm, chunk=N_META)

    chunk = 64
    grow_main = grow_m.reshape(SUBLANES, bp, seq // chunk, chunk).transpose(1, 2, 0, 3)
    buf_main = _pad_rows(xm_meta[N_META - (CONV_W - 1):], SUBLANES, 0, front=True)[None]
    hm_main, c_main, n_main_st, m_main = _mlstm(
        xm_m.reshape(bp, seq, D_MODEL), om_m.reshape(bp, seq, D_MODEL), gcol_m.reshape(bp, seq, LANES),
        grow_main, buf_main, c_meta, n_meta, m_meta, wm, chunk=chunk)

    pad_t = lambda a: _pad_rows(a.reshape(bs, ts, a.shape[-1]), QROWS, 1)
    gcol_pad = jnp.concatenate(
        [gcol_smp.reshape(bs, ts, LANES),
         jnp.broadcast_to(jnp.where(jnp.arange(LANES) < NH_M, NEG_BIG, 0.0).astype(F32), (bs, QROWS - ts, LANES))],
        axis=1)
    grow_pad = jnp.concatenate(
        [grow_smp.reshape(SUBLANES, bs, ts),
         jnp.broadcast_to(jnp.where(jnp.arange(SUBLANES) < NH_M, NEG_BIG, 0.0).astype(F32)[:, None, None],
                          (SUBLANES, bs, QROWS - ts))], axis=2).transpose(1, 0, 2)[:, None]
    hm_smp, c_smp, n_smp, m_smp = _mlstm(
        pad_t(xm_smp), pad_t(om_smp), gcol_pad, grow_pad,
        _pad_rows(state_conv[0], SUBLANES, 1, front=True), state_C[0],
        _pad_rows(state_n[0], SUBLANES, 1), jnp.broadcast_to(_pad_rows(state_m[0], SUBLANES, 1)[:, :, None],
                                                             (bs, SUBLANES, LANES)),
        wm, chunk=QROWS)

    q_smp, q_meta = split(q_s)
    k_smp, k_meta = split(k_s)
    v_smp, v_meta = split(v_s)
    sub = row(subln[0])
    o_meta = _attn_prompt(q_meta[None], k_meta[None], v_meta[None], k_meta, v_meta, lams, sub,
                          tq=N_META, has_prefix=False)
    tq = _row_tile(seq, 256)
    o_main = _attn_prompt(q_m.reshape(bp, seq, D_MODEL), k_m.reshape(bp, seq, D_MODEL),
                          v_m.reshape(bp, seq, D_MODEL), k_meta, v_meta, lams, sub, tq=tq, has_prefix=True)
    del o_meta
    o_smp = _attn_sample(page_table, pad_t(q_smp), cache_k[0].reshape(-1, page, D_MODEL),
                         cache_v[0].reshape(-1, page, D_MODEL), pad_t(k_smp), pad_t(v_smp), lams, sub, n_new=ts)

    x2_main = _merge(hm_main.reshape(n_main, D_MODEL), o_main.reshape(n_main, D_MODEL), ga_m, gb_m, x1_main,
                     wa, wb, wo)
    y_main = _ffn(x2_main, *f2, fin, final_norm=True)
    x2_smp = _merge(hm_smp[:, :ts].reshape(n_samp, D_MODEL), o_smp[:, :ts].reshape(n_samp, D_MODEL),
                    ga_s[:n_samp], gb_s[:n_samp], x1_small[:n_samp], wa, wb, wo)
    y_smp = _ffn(x2_smp, *f2, fin, final_norm=True)
    del hm_meta

    def with_meta(main, meta):
        full = jnp.concatenate([jnp.broadcast_to(meta[None], (bp, N_META, D_MODEL)),
                                main.reshape(bp, seq, D_MODEL)], axis=1)
        return full.reshape(1, bp, seq + N_META, NH_D, DV_D)

    xm_main3 = xm_m.reshape(bp, seq, D_MODEL)
    return (
        y_main.reshape(bp, seq, D_MODEL),
        y_smp.reshape(bs, ts, D_MODEL),
        with_meta(k_m, k_meta),
        with_meta(v_m, v_meta),
        c_main[None],
        n_main_st[None, :, :NH_M, :],
        m_main[None, :, :NH_M, 0],
        xm_main3[None, :, seq - (CONV_W - 1):, :],
        k_smp.reshape(1, bs, ts, NH_D, DV_D),
        v_smp.reshape(1, bs, ts, NH_D, DV_D),
        c_smp[None],
        n_smp[None, :, :NH_M, :],
        m_smp[None, :, :NH_M, 0],
        xm_smp.reshape(bs, ts, D_MODEL)[None, :, ts - (CONV_W - 1):, :],
    )
```

```python
import functools
import math

import jax
import jax.numpy as jnp
from jax import lax
from jax.experimental import pallas as pl
from jax.experimental.pallas import tpu as pltpu

F32 = jnp.float32
BF16 = jnp.bfloat16

D_MODEL = 1024
N_META = 16
EPS = 1e-6
D_FF = 2816
NH_M = 4
DH_M = 256
CONV_W = 4
NH_D = 8
DH_D = 64
DV_D = 128
ROT_DIM = 16
ROPE_THETA = 500000.0
LAMBDA_INIT = 0.8 - 0.6 * math.exp(-0.3 * 0)

LANES = 128
SUBLANES = 8
NEG_BIG = -1e30
VMEM_LIMIT = 56 * 1024 * 1024


def _cparams(sem):
    return pltpu.CompilerParams(dimension_semantics=sem, vmem_limit_bytes=VMEM_LIMIT)


def _rms(x, g):
    return x * lax.rsqrt(jnp.mean(x * x, axis=-1, keepdims=True) + EPS) * g


def _sigmoid(x):
    return 1.0 / (1.0 + jnp.exp(-x))


def _dot(a, b):
    return jnp.dot(a, b, preferred_element_type=F32)


def _dot_nt(a, b):
    return lax.dot_general(a, b, (((1,), (1,)), ((), ())), preferred_element_type=F32)


def _row_tile(n, want):
    t = min(want, n)
    while n % t:
        t //= 2
    assert t == n or t % SUBLANES == 0
    return t


def _full(shape):
    zeros = (0,) * len(shape)
    return pl.BlockSpec(shape, lambda *_: zeros)


def _ffn_body(x_ref, g_ref, wg_ref, wu_ref, wd_ref, fg_ref, o_ref, *, final_norm):
    x = x_ref[...]
    xb = _rms(x, g_ref[...]).astype(BF16)
    g = _dot(xb, wg_ref[...])
    u = _dot(xb, wu_ref[...])
    a = (g * _sigmoid(g) * u).astype(BF16)
    y = x + 0.5 * _dot(a, wd_ref[...])
    if final_norm:
        y = _rms(y, fg_ref[...])
    o_ref[...] = y


def _ffn(x, gain, wg, wu, wd, final_gain, *, final_norm, tm=512):
    n = x.shape[0]
    tm = _row_tile(n, tm)
    row = pl.BlockSpec((tm, D_MODEL), lambda i: (i, 0))
    return pl.pallas_call(
        functools.partial(_ffn_body, final_norm=final_norm),
        grid=(n // tm,),
        in_specs=[row, _full((1, D_MODEL)), _full(wg.shape), _full(wu.shape), _full(wd.shape),
                  _full((1, D_MODEL))],
        out_specs=row,
        out_shape=jax.ShapeDtypeStruct((n, D_MODEL), F32),
        compiler_params=_cparams(("parallel",)),
        name="ffn_final" if final_norm else "ffn",
    )(x, gain, wg, wu, wd, final_gain)


def _rope_slab(xs, c, s_up, s_dn):
    return xs * c + pltpu.roll(xs, LANES - ROT_DIM // 2, axis=1) * s_up + pltpu.roll(xs, ROT_DIM // 2, axis=1) * s_dn


def _inproj_body(x_ref, g_ref, wxm_ref, wom_ref, wq_ref, wk_ref, wv_ref, wga_ref, wgb_ref, wgate_ref,
                 gbias_ref, cq_ref, cu_ref, cd_ref,
                 xm_ref, om_ref, q_ref, k_ref, v_ref, ga_ref, gb_ref, gcol_ref, grow_ref):
    hb = _rms(x_ref[...], g_ref[...]).astype(BF16)
    xm_ref[...] = _dot(hb, wxm_ref[...])
    om_ref[...] = _dot(hb, wom_ref[...])
    v_ref[...] = _dot(hb, wv_ref[...])
    ga_ref[...] = _dot(hb, wga_ref[...])
    gb_ref[...] = _dot(hb, wgb_ref[...])
    c, s_up, s_dn = cq_ref[...], cu_ref[...], cd_ref[...]
    q = _dot(hb, wq_ref[...])
    k = _dot(hb, wk_ref[...])
    qscale = DH_D ** -0.5 * math.log2(math.e)
    for h in range(NH_D):
        sl = slice(h * LANES, (h + 1) * LANES)
        q_ref[:, sl] = (_rope_slab(q[:, sl], c, s_up, s_dn) * qscale).astype(BF16)
        k_ref[:, sl] = _rope_slab(k[:, sl], c, s_up, s_dn)
    pre = _dot(hb, wgate_ref[...]) + gbias_ref[...]
    lane = lax.broadcasted_iota(jnp.int32, pre.shape, 1)
    logsig = jnp.minimum(pre, 0.0) - jnp.log(1.0 + jnp.exp(-jnp.abs(pre)))
    glog = jnp.where(lane < NH_M, pre, logsig)
    gcol_ref[...] = glog
    grow_ref[...] = glog.T[:SUBLANES, :]


def _inproj(x, gain, w, gbias, rope_tabs, *, n_rope_blocks, tm=256):
    n = x.shape[0]
    tm = _row_tile(n, tm)
    row = pl.BlockSpec((tm, D_MODEL), lambda i: (i, 0))
    tab = pl.BlockSpec((tm, LANES), lambda i: (i % n_rope_blocks, 0))
    wspec = _full((D_MODEL, D_MODEL))
    out_shapes = (
        jax.ShapeDtypeStruct((n, D_MODEL), F32),
        jax.ShapeDtypeStruct((n, D_MODEL), F32),
        jax.ShapeDtypeStruct((n, D_MODEL), BF16),
        jax.ShapeDtypeStruct((n, D_MODEL), F32),
        jax.ShapeDtypeStruct((n, D_MODEL), F32),
        jax.ShapeDtypeStruct((n, D_MODEL), F32),
        jax.ShapeDtypeStruct((n, D_MODEL), F32),
        jax.ShapeDtypeStruct((n, LANES), F32),
        jax.ShapeDtypeStruct((SUBLANES, n), F32),
    )
    return pl.pallas_call(
        _inproj_body,
        grid=(n // tm,),
        in_specs=[row, _full((1, D_MODEL))] + [wspec] * 7 + [_full((D_MODEL, LANES)), _full((1, LANES)),
                                                              tab, tab, tab],
        out_specs=(row, row, row, row, row, row, row,
                   pl.BlockSpec((tm, LANES), lambda i: (i, 0)),
                   pl.BlockSpec((SUBLANES, tm), lambda i: (0, i))),
        out_shape=out_shapes,
        compiler_params=_cparams(("parallel",)),
        name="inproj",
    )(x, gain, w["xm"], w["om"], w["q"], w["k"], w["v"], w["ga"], w["gb"], w["gate"], gbias, *rope_tabs)


def _mlstm_body(xm_ref, xprev_ref, buf0_ref, om_ref, gcol_ref, grow_ref, cw_ref, cb_ref,
                wq_ref, wk_ref, wv_ref, wvt_ref, norm_ref, c0_ref, n0_ref, m0_ref,
                hm_ref, c_ref, n_ref, m_ref, ext_ref, *, chunk):
    ci = pl.program_id(1)
    L = chunk

    @pl.when(ci == 0)
    def _():
        c_ref[...] = c0_ref[...]
        n_ref[...] = n0_ref[...]
        m_ref[...] = m0_ref[...]
        ext_ref[0:SUBLANES, :] = buf0_ref[0]

    @pl.when(ci > 0)
    def _():
        ext_ref[0:SUBLANES, :] = xprev_ref[0]

    x = xm_ref[0]
    ext_ref[SUBLANES:SUBLANES + L, :] = x
    cv = cb_ref[...]
    for j in range(CONV_W):
        cv = cv + ext_ref[pl.ds(SUBLANES - (CONV_W - 1) + j, L), :] * cw_ref[j:j + 1, :]
    cact = cv * _sigmoid(cv)

    ti = lax.broadcasted_iota(jnp.int32, (L, L), 0)
    si = lax.broadcasted_iota(jnp.int32, (L, L), 1)
    tril = si <= ti
    gcol = gcol_ref[0]
    grow = grow_ref[0, 0]
    om = om_ref[0]

    for h in range(NH_M):
        hs = slice(h * DH_M, (h + 1) * DH_M)
        chb = cact[:, hs].astype(BF16)
        xhb = x[:, hs].astype(BF16)
        q = _dot(chb, wq_ref[h])
        k = _dot(chb, wk_ref[h])
        v = _dot(xhb, wv_ref[h])
        vt = _dot_nt(wvt_ref[h], xhb)
        qb, kb, vb = q.astype(BF16), k.astype(BF16), v.astype(BF16)

        logi_c, logf_c = gcol[:, h:h + 1], gcol[:, NH_M + h:NH_M + h + 1]
        logi_r, logf_r = grow[h:h + 1, :], grow[NH_M + h:NH_M + h + 1, :]
        b_c = jnp.sum(jnp.where(tril, logf_r, 0.0), axis=1, keepdims=True)
        b_r = jnp.sum(jnp.where(ti <= si, logf_c, 0.0), axis=0, keepdims=True)
        m_prev = m_ref[0, h:h + 1, 0:1]
        n_prev = n_ref[0, h:h + 1, :]
        c_prev = c_ref[0, h]

        d = jnp.where(tril, b_c - b_r + logi_r, -jnp.inf)
        inter = b_c + m_prev
        m_t = jnp.maximum(jnp.max(d, axis=1, keepdims=True), inter)
        w_intra = jnp.exp(d - m_t)
        w_inter = jnp.exp(inter - m_t)
        s = _dot_nt(qb, kb) * w_intra
        num = _dot(s.astype(BF16), vb) + w_inter * _dot_nt(qb, c_prev.astype(BF16))
        den = jnp.sum(s, axis=1, keepdims=True) + w_inter * jnp.sum(q * n_prev, axis=1, keepdims=True)
        hh = num / jnp.maximum(jnp.abs(den), jnp.exp(-m_t))

        m_new = m_t[L - 1:L, :]
        b_last = b_c[L - 1:L, :]
        w_state_r = jnp.exp(b_last - b_r + logi_r - m_new)
        w_state_c = jnp.exp(b_last - b_c + logi_c - m_new)
        decay = jnp.exp(b_last + m_prev - m_new)
        c_ref[0, h] = decay * c_prev + _dot((vt * w_state_r).astype(BF16), kb)
        n_ref[0, h:h + 1, :] = decay * n_prev + jnp.sum(k * w_state_c, axis=0, keepdims=True)
        m_ref[0, h:h + 1, :] = jnp.broadcast_to(m_new, (1, LANES))

        hn = hh * lax.rsqrt(jnp.mean(hh * hh, axis=1, keepdims=True) + EPS) * norm_ref[:, hs]
        hm_ref[0, :, hs] = (hn * _sigmoid(om[:, hs])).astype(BF16)


def _mlstm(xm, om, gcol, grow, buf0, c0, n0, m0, w, *, chunk):
    B, T, _ = xm.shape
    L = chunk
    nc = T // L
    shared = c0.shape[0] == 1
    st = (lambda b, c: (0, 0, 0)) if shared else (lambda b, c: (b, 0, 0))
    st4 = (lambda b, c: (0, 0, 0, 0)) if shared else (lambda b, c: (b, 0, 0, 0))
    seq = pl.BlockSpec((1, L, D_MODEL), lambda b, c: (b, c, 0))
    wsq = _full((NH_M, DH_M, DH_M))
    return pl.pallas_call(
        functools.partial(_mlstm_body, chunk=L),
        grid=(B, nc),
        in_specs=[
            seq,
            pl.BlockSpec((1, SUBLANES, D_MODEL), lambda b, c: (b, jnp.maximum(c * (L // SUBLANES) - 1, 0), 0)),
            pl.BlockSpec((1, SUBLANES, D_MODEL), st),
            seq,
            pl.BlockSpec((1, L, LANES), lambda b, c: (b, c, 0)),
            pl.BlockSpec((1, 1, SUBLANES, L), lambda b, c: (b, c, 0, 0)),
            _full((CONV_W, D_MODEL)), _full((1, D_MODEL)),
            wsq, wsq, wsq, wsq, _full((1, D_MODEL)),
            pl.BlockSpec((1, NH_M, DH_M, DH_M), st4),
            pl.BlockSpec((1, SUBLANES, DH_M), st),
            pl.BlockSpec((1, SUBLANES, LANES), st),
        ],
        out_specs=(
            seq,
            pl.BlockSpec((1, NH_M, DH_M, DH_M), lambda b, c: (b, 0, 0, 0)),
            pl.BlockSpec((1, SUBLANES, DH_M), lambda b, c: (b, 0, 0)),
            pl.BlockSpec((1, SUBLANES, LANES), lambda b, c: (b, 0, 0)),
        ),
        out_shape=(
            jax.ShapeDtypeStruct((B, T, D_MODEL), BF16),
            jax.ShapeDtypeStruct((B, NH_M, DH_M, DH_M), F32),
            jax.ShapeDtypeStruct((B, SUBLANES, DH_M), F32),
            jax.ShapeDtypeStruct((B, SUBLANES, LANES), F32),
        ),
        scratch_shapes=[pltpu.VMEM((SUBLANES + L, D_MODEL), F32)],
        compiler_params=_cparams(("parallel", "arbitrary")),
        name="mlstm",
    )(xm, xm, buf0, om, gcol, grow, w["conv_w"], w["conv_b"], w["mq"], w["mk"], w["mv"], w["mvt"],
      w["mnorm"], c0, n0, m0)


def _lam(lq1_ref, lk1_ref, lq2_ref, lk2_ref):
    return (jnp.exp(jnp.sum(lq1_ref[...] * lk1_ref[...], axis=1, keepdims=True))
            - jnp.exp(jnp.sum(lq2_ref[...] * lk2_ref[...], axis=1, keepdims=True)) + LAMBDA_INIT)


def _online_softmax_step(m_ref, l_ref, acc_ref, s, vb):
    m_old = m_ref[...]
    m_new = jnp.maximum(m_old, jnp.max(s, axis=1, keepdims=True))
    pr = jnp.exp2(s - m_new)
    r = jnp.exp2(m_old - m_new)
    l_ref[...] = r * l_ref[...] + jnp.sum(pr, axis=1, keepdims=True)
    acc_ref[...] = r * acc_ref[...] + _dot(pr.astype(BF16), vb)
    m_ref[...] = m_new


def _attn_prompt_body(q_ref, k_ref, v_ref, kp_ref, vpt_ref, lq1_ref, lk1_ref, lq2_ref, lk2_ref, subln_ref,
                      o_ref, kb_ref, vt_ref, m_ref, l_ref, acc_ref, *, tq, has_prefix):
    qi = pl.program_id(2)
    nblk = vt_ref.shape[0]

    @pl.when(qi == 0)
    def _():
        kb_ref[...] = k_ref[0].astype(BF16)
        for c in range(nblk):
            vt_ref[c] = v_ref[0, c * tq:(c + 1) * tq, :].T.astype(BF16)

    q = q_ref[0].astype(F32)
    lane = lax.broadcasted_iota(jnp.int32, q.shape, 1)
    qq = jnp.concatenate([jnp.where(lane < DH_D, q, 0.0), jnp.where(lane < DH_D, 0.0, q)], axis=0)
    qqt = qq.T.astype(BF16)
    m_ref[...] = jnp.full(m_ref.shape, -jnp.inf, F32)
    l_ref[...] = jnp.zeros(l_ref.shape, F32)
    acc_ref[...] = jnp.zeros(acc_ref.shape, F32)

    def step(st, vt):
        m_old = m_ref[...]
        m_new = jnp.maximum(m_old, jnp.max(st, axis=0, keepdims=True))
        pr = jnp.exp2(st - m_new)
        r = jnp.exp2(m_old - m_new)
        l_ref[...] = r * l_ref[...] + jnp.sum(pr, axis=0, keepdims=True)
        acc_ref[...] = r * acc_ref[...] + _dot(vt, pr.astype(BF16))
        m_ref[...] = m_new

    st = _dot(kb_ref[pl.ds(pl.multiple_of(qi * tq, tq), tq), :], qqt)
    krel = lax.broadcasted_iota(jnp.int32, st.shape, 0)
    qrel = lax.broadcasted_iota(jnp.int32, st.shape, 1) % tq
    step(jnp.where(krel <= qrel, st, -jnp.inf), vt_ref[qi])

    if has_prefix:
        step(_dot(kp_ref[...].astype(BF16), qqt), vpt_ref[...].astype(BF16))

    def body(j, carry):
        step(_dot(kb_ref[pl.ds(pl.multiple_of(j * tq, tq), tq), :], qqt), vt_ref[j])
        return carry

    lax.fori_loop(0, qi, body, 0)
    lam = _lam(lq1_ref, lk1_ref, lq2_ref, lk2_ref)
    ot = acc_ref[:, 0:tq] / l_ref[:, 0:tq] - lam * (acc_ref[:, tq:2 * tq] / l_ref[:, tq:2 * tq])
    ot = ot * lax.rsqrt(jnp.mean(ot * ot, axis=0, keepdims=True) + EPS) * subln_ref[...]
    o_ref[0] = (ot * (1.0 - LAMBDA_INIT)).T.astype(BF16)


def _attn_prompt(q, k, v, kpre, vpre_t, lams, subln_col, *, tq, has_prefix):
    B, T, _ = q.shape
    P = kpre.shape[0]
    lspec = _full((1, DH_D))
    return pl.pallas_call(
        functools.partial(_attn_prompt_body, tq=tq, has_prefix=has_prefix),
        grid=(B, NH_D, T // tq),
        in_specs=[
            pl.BlockSpec((1, tq, LANES), lambda b, h, i: (b, i, h)),
            pl.BlockSpec((1, T, LANES), lambda b, h, i: (b, 0, h)),
            pl.BlockSpec((1, T, LANES), lambda b, h, i: (b, 0, h)),
            pl.BlockSpec((P, LANES), lambda b, h, i: (0, h)),
            pl.BlockSpec((DV_D, P), lambda b, h, i: (h, 0)),
            lspec, lspec, lspec, lspec, _full((DV_D, 1)),
        ],
        out_specs=pl.BlockSpec((1, tq, LANES), lambda b, h, i: (b, i, h)),
        out_shape=jax.ShapeDtypeStruct((B, T, D_MODEL), BF16),
        scratch_shapes=[pltpu.VMEM((T, LANES), BF16), pltpu.VMEM((T // tq, DV_D, tq), BF16),
                        pltpu.VMEM((1, 2 * tq), F32), pltpu.VMEM((1, 2 * tq), F32),
                        pltpu.VMEM((DV_D, 2 * tq), F32)],
        compiler_params=_cparams(("parallel", "parallel", "arbitrary")),
        name="attn_prompt",
    )(q, k, v, kpre, vpre_t, *lams, subln_col)


QROWS = 8
QMAT_ROWS = 2 * NH_D * QROWS


def _attn_sample_body(pt_ref, q_ref, kc_ref, vc_ref, kn_ref, vn_ref, lq1_ref, lk1_ref, lq2_ref, lk2_ref,
                      subln_ref, o_ref, qm_ref, bias_ref, m_ref, l_ref, acc_ref, *, n_new):
    p = pl.program_id(1)
    R = QMAT_ROWS
    rows = kc_ref.shape[1]
    half = R // 2

    def head_of(r):
        return (r % half) // QROWS

    @pl.when(p == 0)
    def _():
        q = q_ref[0].astype(F32)
        lane = lax.broadcasted_iota(jnp.int32, (QROWS, LANES), 1)
        lo, hi = [], []
        for h in range(NH_D):
            slab = q[:, h * LANES:(h + 1) * LANES]
            lo.append(jnp.where(lane < DH_D, slab, 0.0))
            hi.append(jnp.where(lane < DH_D, 0.0, slab))
        qm_ref[...] = jnp.concatenate(lo + hi, axis=0).astype(BF16)
        row = lax.broadcasted_iota(jnp.int32, (R, rows), 0)
        col = lax.broadcasted_iota(jnp.int32, (R, rows), 1)
        bias_ref[...] = jnp.where(col % NH_D == head_of(row), 0.0, -jnp.inf)
        m_ref[...] = jnp.full(m_ref.shape, -jnp.inf, F32)
        l_ref[...] = jnp.zeros(l_ref.shape, F32)
        acc_ref[...] = jnp.zeros(acc_ref.shape, F32)

    step = functools.partial(_online_softmax_step, m_ref, l_ref, acc_ref)
    qm = qm_ref[...]
    step(_dot_nt(qm, kc_ref[0].astype(BF16)) + bias_ref[...], vc_ref[0].astype(BF16))

    @pl.when(p == pl.num_programs(1) - 1)
    def _():
        zpad = jnp.zeros((LANES - kn_ref.shape[1], LANES), F32)
        kn = jnp.concatenate([kn_ref[0], zpad], axis=0).astype(BF16)
        vn = jnp.concatenate([vn_ref[0], zpad], axis=0).astype(BF16)
        s = _dot_nt(qm, kn)
        row = lax.broadcasted_iota(jnp.int32, s.shape, 0)
        col = lax.broadcasted_iota(jnp.int32, s.shape, 1)
        tok = col // NH_D
        ok = (col % NH_D == head_of(row)) & (tok <= row % QROWS) & (tok < n_new)
        step(jnp.where(ok, s, -jnp.inf), vn)
        lam = _lam(lq1_ref, lk1_ref, lq2_ref, lk2_ref)
        o = acc_ref[0:half, :] / l_ref[0:half, :] - lam * (acc_ref[half:R, :] / l_ref[half:R, :])
        o_ref[0] = (_rms(o, subln_ref[...]) * (1.0 - LAMBDA_INIT)).astype(BF16)


def _attn_sample(page_table, q, cache_k, cache_v, knew, vnew, lams, subln, *, n_new):
    B, n_pages = page_table.shape
    rows = cache_k.shape[1]
    R = QMAT_ROWS
    lspec = pl.BlockSpec((1, DH_D), lambda b, p, pt: (0, 0))
    pg = pl.BlockSpec((1, rows, LANES), lambda b, p, pt: (pt[b, p], 0, 0))
    new = pl.BlockSpec((1, n_new * NH_D, LANES), lambda b, p, pt: (b, 0, 0))
    return pl.pallas_call(
        functools.partial(_attn_sample_body, n_new=n_new),
        grid_spec=pltpu.PrefetchScalarGridSpec(
            num_scalar_prefetch=1,
            grid=(B, n_pages),
            in_specs=[pl.BlockSpec((1, QROWS, D_MODEL), lambda b, p, pt: (b, 0, 0)), pg, pg, new, new,
                      lspec, lspec, lspec, lspec, pl.BlockSpec((1, DV_D), lambda b, p, pt: (0, 0))],
            out_specs=pl.BlockSpec((1, R // 2, DV_D), lambda b, p, pt: (b, 0, 0)),
            scratch_shapes=[pltpu.VMEM((R, LANES), BF16), pltpu.VMEM((R, rows), F32), pltpu.VMEM((R, 1), F32),
                            pltpu.VMEM((R, 1), F32), pltpu.VMEM((R, DV_D), F32)],
        ),
        out_shape=jax.ShapeDtypeStruct((B, R // 2, DV_D), BF16),
        compiler_params=_cparams(("parallel", "arbitrary")),
        name="attn_sample",
    )(page_table, q, cache_k, cache_v, knew, vnew, *lams, subln)


def _merge_body(hm_ref, o_ref, ga_ref, gb_ref, x_ref, wa_ref, wb_ref, wo_ref, y_ref):
    ya = _dot(hm_ref[...], wa_ref[...])
    yb = _dot(o_ref[...], wb_ref[...])
    merged = _sigmoid(ga_ref[...]) * ya + _sigmoid(gb_ref[...]) * yb
    y_ref[...] = x_ref[...] + _dot(merged.astype(BF16), wo_ref[...])


def _merge(hm, o, ga, gb, x, wa, wb, wo, *, tm=512):
    n = x.shape[0]
    tm = _row_tile(n, tm)
    row = pl.BlockSpec((tm, D_MODEL), lambda i: (i, 0))
    wspec = _full((D_MODEL, D_MODEL))
    return pl.pallas_call(
        _merge_body,
        grid=(n // tm,),
        in_specs=[row, row, row, row, row, wspec, wspec, wspec],
        out_specs=row,
        out_shape=jax.ShapeDtypeStruct((n, D_MODEL), F32),
        compiler_params=_cparams(("parallel",)),
        name="merge",
    )(hm, o, ga, gb, x, wa, wb, wo)


def _rope_tables(pos):
    half = ROT_DIM // 2
    inv_freq = ROPE_THETA ** (-jnp.arange(half, dtype=F32) / half)
    ang = pos.astype(F32)[:, None] * inv_freq[None, :]
    cos, sin = jnp.cos(ang), jnp.sin(ang)
    n = pos.shape[0]
    ones = jnp.ones((n, DH_D - ROT_DIM), F32)
    zeros = jnp.zeros((n, DH_D - ROT_DIM), F32)
    zh = jnp.zeros((n, half), F32)
    c = jnp.concatenate([cos, cos, ones] * 2, axis=1)
    s_up = jnp.concatenate([-sin, zh, zeros] * 2, axis=1)
    s_dn = jnp.concatenate([zh, sin, zeros] * 2, axis=1)
    return c, s_up, s_dn


def _pad_rows(a, rows, axis, front=False):
    pad = [(0, 0)] * a.ndim
    pad[axis] = (rows - a.shape[axis], 0) if front else (0, rows - a.shape[axis])
    return jnp.pad(a, pad)


def kernel(x_prompt, x_sample, cache_k, cache_v, page_table, state_C, state_n, state_m, state_conv, meta_tokens, ffn1_norm, ffn1_w_gate, ffn1_w_up, ffn1_w_down, mix_norm, w_in, conv_w, conv_b, w_mq, w_mk, w_mv, b_igate, b_fgate, mlstm_norm, lambda_q1, lambda_k1, lambda_q2, lambda_k2, subln, w_proj_a, w_proj_b, w_out, ffn2_norm, ffn2_w_gate, ffn2_w_up, ffn2_w_down, final_norm):
    bp, seq, _ = x_prompt.shape
    bs, ts, _ = x_sample.shape
    n_pages = page_table.shape[1]
    page = cache_k.shape[2]
    past_len = n_pages * page
    assert w_in.shape[0] == 1 and ts <= QROWS

    win = w_in[0]
    offs = [0]
    for width in (D_MODEL, D_MODEL, NH_M, NH_M, D_MODEL, D_MODEL, D_MODEL, D_MODEL, D_MODEL):
        offs.append(offs[-1] + width)
    col = lambda i: win[:, offs[i]:offs[i + 1]]
    wgate = jnp.pad(jnp.concatenate([col(2), col(3)], axis=1), ((0, 0), (0, LANES - 2 * NH_M)))
    wproj = {"xm": col(0).astype(BF16), "om": col(1).astype(BF16), "q": col(4).astype(BF16),
             "k": col(5).astype(BF16), "v": col(6).astype(BF16), "ga": col(7).astype(BF16),
             "gb": col(8).astype(BF16), "gate": wgate.astype(BF16)}
    gbias = jnp.pad(jnp.concatenate([b_igate[0], b_fgate[0]])[None, :], ((0, 0), (0, LANES - 2 * NH_M)))
    wm = {"conv_w": conv_w[0], "conv_b": conv_b[0][None, :],
          "mq": w_mq[0].astype(BF16), "mk": (w_mk[0] * DH_M ** -0.5).astype(BF16), "mv": w_mv[0].astype(BF16),
          "mvt": jnp.swapaxes(w_mv[0], 1, 2).astype(BF16), "mnorm": mlstm_norm[0].reshape(1, D_MODEL)}
    lams = (lambda_q1, lambda_k1, lambda_q2, lambda_k2)
    row = lambda a: a.reshape(1, -1)
    f1 = (row(ffn1_norm[0]), ffn1_w_gate[0].astype(BF16), ffn1_w_up[0].astype(BF16), ffn1_w_down[0].astype(BF16))
    f2 = (row(ffn2_norm[0]), ffn2_w_gate[0].astype(BF16), ffn2_w_up[0].astype(BF16), ffn2_w_down[0].astype(BF16))
    fin = row(final_norm)
    wa, wb, wo = w_proj_a[0].astype(BF16), w_proj_b[0].astype(BF16), w_out[0].astype(BF16)

    n_main = bp * seq
    n_samp = bs * ts
    x_main = x_prompt.reshape(n_main, D_MODEL)
    x_small = jnp.concatenate([x_sample.reshape(n_samp, D_MODEL), meta_tokens], axis=0)

    tm_in = _row_tile(seq, 256)
    tabs_main = _rope_tables(N_META + jnp.arange(seq, dtype=jnp.int32))
    pos_small = jnp.concatenate([jnp.tile(past_len + jnp.arange(ts, dtype=jnp.int32), bs),
                                 jnp.arange(N_META, dtype=jnp.int32)])
    tabs_small = _rope_tables(pos_small)

    x1_main = _ffn(x_main, *f1, fin, final_norm=False)
    x1_small = _ffn(x_small, *f1, fin, final_norm=False)
    pm = _inproj(x1_main, row(mix_norm[0]), wproj, gbias, tabs_main, n_rope_blocks=seq // tm_in, tm=tm_in)
    ps = _inproj(x1_small, row(mix_norm[0]), wproj, gbias, tabs_small, n_rope_blocks=1, tm=x_small.shape[0])
    xm_m, om_m, q_m, k_m, v_m, ga_m, gb_m, gcol_m, grow_m = pm
    xm_s, om_s, q_s, k_s, v_s, ga_s, gb_s, gcol_s, grow_s = ps

    def split(a):
        return a[:n_samp], a[n_samp:]

    xm_smp, xm_meta = split(xm_s)
    om_smp, om_meta = split(om_s)
    gcol_smp, gcol_meta = split(gcol_s)
    grow_smp, grow_meta = grow_s[:, :n_samp], grow_s[:, n_samp:]

    zc = jnp.zeros((1, NH_M, DH_M, DH_M), F32)
    zn = jnp.zeros((1, SUBLANES, DH_M), F32)
    zm = jnp.zeros((1, SUBLANES, LANES), F32)
    zbuf = jnp.zeros((1, SUBLANES, D_MODEL), F32)
    hm_meta, c_meta, n_meta, m_meta = _mlstm(
        xm_meta[None], om_meta[None], gcol_meta[None], grow_meta.reshape(1, 1, SUBLANES, N_META),
        zbuf, zc, zn, zm, wm, chunk=N_META)

    chunk = 64
    grow_main = grow_m.reshape(SUBLANES, bp, seq // chunk, chunk).transpose(1, 2, 0, 3)
    buf_main = _pad_rows(xm_meta[N_META - (CONV_W - 1):], SUBLANES, 0, front=True)[None]
    hm_main, c_main, n_main_st, m_main = _mlstm(
        xm_m.reshape(bp, seq, D_MODEL), om_m.reshape(bp, seq, D_MODEL), gcol_m.reshape(bp, seq, LANES),
        grow_main, buf_main, c_meta, n_meta, m_meta, wm, chunk=chunk)

    pad_t = lambda a: _pad_rows(a.reshape(bs, ts, a.shape[-1]), QROWS, 1)
    gcol_pad = jnp.concatenate(
        [gcol_smp.reshape(bs, ts, LANES),
         jnp.broadcast_to(jnp.where(jnp.arange(LANES) < NH_M, NEG_BIG, 0.0).astype(F32), (bs, QROWS - ts, LANES))],
        axis=1)
    grow_pad = jnp.concatenate(
        [grow_smp.reshape(SUBLANES, bs, ts),
         jnp.broadcast_to(jnp.where(jnp.arange(SUBLANES) < NH_M, NEG_BIG, 0.0).astype(F32)[:, None, None],
                          (SUBLANES, bs, QROWS - ts))], axis=2).transpose(1, 0, 2)[:, None]
    hm_smp, c_smp, n_smp, m_smp = _mlstm(
        pad_t(xm_smp), pad_t(om_smp), gcol_pad, grow_pad,
        _pad_rows(state_conv[0], SUBLANES, 1, front=True), state_C[0],
        _pad_rows(state_n[0], SUBLANES, 1), jnp.broadcast_to(_pad_rows(state_m[0], SUBLANES, 1)[:, :, None],
                                                             (bs, SUBLANES, LANES)),
        wm, chunk=QROWS)

    q_smp, q_meta = split(q_s)
    k_smp, k_meta = split(k_s)
    v_smp, v_meta = split(v_s)
    sub = row(subln[0])
    sub_col = subln[0].reshape(DV_D, 1)
    o_meta = _attn_prompt(q_meta[None], k_meta[None], v_meta[None], k_meta, v_meta.T, lams, sub_col,
                          tq=N_META, has_prefix=False)
    tq = _row_tile(seq, 512)
    o_main = _attn_prompt(q_m.reshape(bp, seq, D_MODEL), k_m.reshape(bp, seq, D_MODEL),
                          v_m.reshape(bp, seq, D_MODEL), k_meta, v_meta.T, lams, sub_col, tq=tq, has_prefix=True)
    del o_meta
    o_smp = _attn_sample(page_table, pad_t(q_smp), cache_k.reshape(-1, page * NH_D, DV_D),
                         cache_v.reshape(-1, page * NH_D, DV_D), k_smp.reshape(bs, ts * NH_D, DV_D),
                         v_smp.reshape(bs, ts * NH_D, DV_D), lams, sub, n_new=ts)
    o_smp = o_smp.reshape(bs, NH_D, QROWS, DV_D)[:, :, :ts].transpose(0, 2, 1, 3).reshape(n_samp, D_MODEL)

    x2_main = _merge(hm_main.reshape(n_main, D_MODEL), o_main.reshape(n_main, D_MODEL), ga_m, gb_m, x1_main,
                     wa, wb, wo)
    y_main = _ffn(x2_main, *f2, fin, final_norm=True)
    x2_smp = _merge(hm_smp[:, :ts].reshape(n_samp, D_MODEL), o_smp,
                    ga_s[:n_samp], gb_s[:n_samp], x1_small[:n_samp], wa, wb, wo)
    y_smp = _ffn(x2_smp, *f2, fin, final_norm=True)
    del hm_meta

    def with_meta(main, meta):
        full = jnp.concatenate([jnp.broadcast_to(meta[None], (bp, N_META, D_MODEL)),
                                main.reshape(bp, seq, D_MODEL)], axis=1)
        return full.reshape(1, bp, seq + N_META, NH_D, DV_D)

    xm_main3 = xm_m.reshape(bp, seq, D_MODEL)
    return (
        y_main.reshape(bp, seq, D_MODEL),
        y_smp.reshape(bs, ts, D_MODEL),
        with_meta(k_m, k_meta),
        with_meta(v_m, v_meta),
        c_main[None],
        n_main_st[None, :, :NH_M, :],
        m_main[None, :, :NH_M, 0],
        xm_main3[None, :, seq - (CONV_W - 1):, :],
        k_smp.reshape(1, bs, ts, NH_D, DV_D),
        v_smp.reshape(1, bs, ts, NH_D, DV_D),
        c_smp[None],
        n_smp[None, :, :NH_M, :],
        m_smp[None, :, :NH_M, 0],
        xm_smp.reshape(bs, ts, D_MODEL)[None, :, ts - (CONV_W - 1):, :],
    )
```

```python
import functools
import math

import jax
import jax.numpy as jnp
from jax import lax
from jax.experimental import pallas as pl
from jax.experimental.pallas import tpu as pltpu

F32 = jnp.float32
BF16 = jnp.bfloat16

D_MODEL = 1024
N_META = 16
EPS = 1e-6
D_FF = 2816
NH_M = 4
DH_M = 256
CONV_W = 4
NH_D = 8
DH_D = 64
DV_D = 128
ROT_DIM = 16
ROPE_THETA = 500000.0
LAMBDA_INIT = 0.8 - 0.6 * math.exp(-0.3 * 0)

LANES = 128
SUBLANES = 8
NEG_BIG = -1e30
VMEM_LIMIT = 56 * 1024 * 1024


def _cparams(sem):
    return pltpu.CompilerParams(dimension_semantics=sem, vmem_limit_bytes=VMEM_LIMIT)


def _rms(x, g):
    return x * lax.rsqrt(jnp.mean(x * x, axis=-1, keepdims=True) + EPS) * g


def _sigmoid(x):
    return 1.0 / (1.0 + jnp.exp(-x))


def _dot(a, b):
    return jnp.dot(a, b, preferred_element_type=F32)


def _dot_nt(a, b):
    return lax.dot_general(a, b, (((1,), (1,)), ((), ())), preferred_element_type=F32)


def _row_tile(n, want):
    t = min(want, n)
    while n % t:
        t //= 2
    assert t == n or t % SUBLANES == 0
    return t


MLSTM_CHUNK = 256
MLSTM_GROUP = 4


def _group(n):
    return max(g for g in range(1, MLSTM_GROUP + 1) if n % g == 0)


def _full(shape):
    zeros = (0,) * len(shape)
    return pl.BlockSpec(shape, lambda *_: zeros)


def _ffn_body(x_ref, g_ref, wg_ref, wu_ref, wd_ref, fg_ref, o_ref, *, final_norm):
    x = x_ref[...]
    xb = _rms(x, g_ref[...]).astype(BF16)
    g = _dot(xb, wg_ref[...])
    u = _dot(xb, wu_ref[...])
    a = (g * _sigmoid(g) * u).astype(BF16)
    y = x + 0.5 * _dot(a, wd_ref[...])
    if final_norm:
        y = _rms(y, fg_ref[...])
    o_ref[...] = y


def _ffn(x, gain, wg, wu, wd, final_gain, *, final_norm, tm=512):
    n = x.shape[0]
    tm = _row_tile(n, tm)
    row = pl.BlockSpec((tm, D_MODEL), lambda i: (i, 0))
    return pl.pallas_call(
        functools.partial(_ffn_body, final_norm=final_norm),
        grid=(n // tm,),
        in_specs=[row, _full((1, D_MODEL)), _full(wg.shape), _full(wu.shape), _full(wd.shape),
                  _full((1, D_MODEL))],
        out_specs=row,
        out_shape=jax.ShapeDtypeStruct((n, D_MODEL), F32),
        compiler_params=_cparams(("parallel",)),
        name="ffn_final" if final_norm else "ffn",
    )(x, gain, wg, wu, wd, final_gain)


def _rope_slab(xs, c, s_up, s_dn):
    return xs * c + pltpu.roll(xs, LANES - ROT_DIM // 2, axis=1) * s_up + pltpu.roll(xs, ROT_DIM // 2, axis=1) * s_dn


def _inproj_body(x_ref, g_ref, wxm_ref, wom_ref, wq_ref, wk_ref, wv_ref, wga_ref, wgb_ref, wgate_ref,
                 gbias_ref, cq_ref, cu_ref, cd_ref,
                 xm_ref, om_ref, q_ref, k_ref, v_ref, ga_ref, gb_ref, gcol_ref, grow_ref):
    hb = _rms(x_ref[...], g_ref[...]).astype(BF16)
    xm_ref[...] = _dot(hb, wxm_ref[...])
    om_ref[...] = _dot(hb, wom_ref[...])
    v_ref[...] = _dot(hb, wv_ref[...])
    ga_ref[...] = _dot(hb, wga_ref[...])
    gb_ref[...] = _dot(hb, wgb_ref[...])
    c, s_up, s_dn = cq_ref[...], cu_ref[...], cd_ref[...]
    q = _dot(hb, wq_ref[...])
    k = _dot(hb, wk_ref[...])
    qscale = DH_D ** -0.5 * math.log2(math.e)
    for h in range(NH_D):
        sl = slice(h * LANES, (h + 1) * LANES)
        q_ref[:, sl] = (_rope_slab(q[:, sl], c, s_up, s_dn) * qscale).astype(BF16)
        k_ref[:, sl] = _rope_slab(k[:, sl], c, s_up, s_dn)
    pre = _dot(hb, wgate_ref[...]) + gbias_ref[...]
    lane = lax.broadcasted_iota(jnp.int32, pre.shape, 1)
    logsig = jnp.minimum(pre, 0.0) - jnp.log(1.0 + jnp.exp(-jnp.abs(pre)))
    glog = jnp.where(lane < NH_M, pre, logsig)
    gcol_ref[...] = glog
    grow_ref[...] = glog.T[:SUBLANES, :]


def _inproj(x, gain, w, gbias, rope_tabs, *, n_rope_blocks, tm=256):
    n = x.shape[0]
    tm = _row_tile(n, tm)
    row = pl.BlockSpec((tm, D_MODEL), lambda i: (i, 0))
    tab = pl.BlockSpec((tm, LANES), lambda i: (i % n_rope_blocks, 0))
    wspec = _full((D_MODEL, D_MODEL))
    out_shapes = (
        jax.ShapeDtypeStruct((n, D_MODEL), F32),
        jax.ShapeDtypeStruct((n, D_MODEL), F32),
        jax.ShapeDtypeStruct((n, D_MODEL), BF16),
        jax.ShapeDtypeStruct((n, D_MODEL), F32),
        jax.ShapeDtypeStruct((n, D_MODEL), F32),
        jax.ShapeDtypeStruct((n, D_MODEL), F32),
        jax.ShapeDtypeStruct((n, D_MODEL), F32),
        jax.ShapeDtypeStruct((n, LANES), F32),
        jax.ShapeDtypeStruct((SUBLANES, n), F32),
    )
    return pl.pallas_call(
        _inproj_body,
        grid=(n // tm,),
        in_specs=[row, _full((1, D_MODEL))] + [wspec] * 7 + [_full((D_MODEL, LANES)), _full((1, LANES)),
                                                              tab, tab, tab],
        out_specs=(row, row, row, row, row, row, row,
                   pl.BlockSpec((tm, LANES), lambda i: (i, 0)),
                   pl.BlockSpec((SUBLANES, tm), lambda i: (0, i))),
        out_shape=out_shapes,
        compiler_params=_cparams(("parallel",)),
        name="inproj",
    )(x, gain, w["xm"], w["om"], w["q"], w["k"], w["v"], w["ga"], w["gb"], w["gate"], gbias, *rope_tabs)


def _mlstm_body(xm_ref, xprev_ref, buf0_ref, om_ref, gcol_ref, grow_ref, cw_ref, cb_ref,
                wq_ref, wk_ref, wv_ref, wvt_ref, norm_ref, c0_ref, n0_ref, m0_ref,
                hm_ref, c_ref, n_ref, m_ref, ext_ref, *, chunk, shared_init):
    ci = pl.program_id(1)
    L = chunk
    nb = xm_ref.shape[0]

    @pl.when(ci == 0)
    def _():
        for bi in range(nb):
            src = 0 if shared_init else bi
            c_ref[bi] = c0_ref[src]
            n_ref[bi] = n0_ref[src]
            m_ref[bi] = m0_ref[src]
            ext_ref[bi, 0:SUBLANES, :] = buf0_ref[src]

    @pl.when(ci > 0)
    def _():
        ext_ref[:, 0:SUBLANES, :] = xprev_ref[...]

    x = xm_ref[...]
    ext_ref[:, SUBLANES:SUBLANES + L, :] = x
    cv = cb_ref[...]
    for j in range(CONV_W):
        cv = cv + ext_ref[:, pl.ds(SUBLANES - (CONV_W - 1) + j, L), :] * cw_ref[j:j + 1, :]
    cact = cv * _sigmoid(cv)

    ti = lax.broadcasted_iota(jnp.int32, (L, L), 0)
    si = lax.broadcasted_iota(jnp.int32, (L, L), 1)
    tril = si <= ti

    for h in range(NH_M):
        hs = slice(h * DH_M, (h + 1) * DH_M)
        chb = cact[:, :, hs].reshape(nb * L, DH_M).astype(BF16)
        xhb = x[:, :, hs].reshape(nb * L, DH_M).astype(BF16)
        q_all = _dot(chb, wq_ref[h])
        k_all = _dot(chb, wk_ref[h])
        v_all = _dot(xhb, wv_ref[h])
        for bi in range(nb):
            rows = slice(bi * L, (bi + 1) * L)
            q, k, v = q_all[rows], k_all[rows], v_all[rows]
            vt = _dot_nt(wvt_ref[h], xhb[rows])
            qb, kb, vb = q.astype(BF16), k.astype(BF16), v.astype(BF16)

            gcol = gcol_ref[bi]
            grow = grow_ref[bi, 0]
            logi_c, logf_c = gcol[:, h:h + 1], gcol[:, NH_M + h:NH_M + h + 1]
            logi_r, logf_r = grow[h:h + 1, :], grow[NH_M + h:NH_M + h + 1, :]
            b_c = jnp.sum(jnp.where(tril, logf_r, 0.0), axis=1, keepdims=True)
            b_r = jnp.sum(jnp.where(ti <= si, logf_c, 0.0), axis=0, keepdims=True)
            m_prev = m_ref[bi, h:h + 1, 0:1]
            n_prev = n_ref[bi, h:h + 1, :]
            c_prev = c_ref[bi, h]

            d = jnp.where(tril, b_c - b_r + logi_r, -jnp.inf)
            inter = b_c + m_prev
            m_t = jnp.maximum(jnp.max(d, axis=1, keepdims=True), inter)
            w_intra = jnp.exp(d - m_t)
            w_inter = jnp.exp(inter - m_t)
            s = _dot_nt(qb, kb) * w_intra
            num = _dot(s.astype(BF16), vb) + w_inter * _dot_nt(qb, c_prev.astype(BF16))
            den = jnp.sum(s, axis=1, keepdims=True) + w_inter * jnp.sum(q * n_prev, axis=1, keepdims=True)
            hh = num / jnp.maximum(jnp.abs(den), jnp.exp(-m_t))

            m_new = m_t[L - 1:L, :]
            b_last = b_c[L - 1:L, :]
            w_state_r = jnp.exp(b_last - b_r + logi_r - m_new)
            w_state_c = jnp.exp(b_last - b_c + logi_c - m_new)
            decay = jnp.exp(b_last + m_prev - m_new)
            c_ref[bi, h] = decay * c_prev + _dot((vt * w_state_r).astype(BF16), kb)
            n_ref[bi, h:h + 1, :] = decay * n_prev + jnp.sum(k * w_state_c, axis=0, keepdims=True)
            m_ref[bi, h:h + 1, :] = jnp.broadcast_to(m_new, (1, LANES))

            hn = hh * lax.rsqrt(jnp.mean(hh * hh, axis=1, keepdims=True) + EPS) * norm_ref[:, hs]
            hm_ref[bi, :, hs] = (hn * _sigmoid(om_ref[bi, :, hs])).astype(BF16)


def _mlstm(xm, om, gcol, grow, buf0, c0, n0, m0, w, *, chunk, nb):
    B, T, _ = xm.shape
    L = chunk
    nc = T // L
    shared = c0.shape[0] == 1 and B > 1
    nb0 = 1 if shared else nb
    st = (lambda b, c: (0, 0, 0)) if shared else (lambda b, c: (b, 0, 0))
    st4 = (lambda b, c: (0, 0, 0, 0)) if shared else (lambda b, c: (b, 0, 0, 0))
    seq = pl.BlockSpec((nb, L, D_MODEL), lambda b, c: (b, c, 0))
    wsq = _full((NH_M, DH_M, DH_M))
    return pl.pallas_call(
        functools.partial(_mlstm_body, chunk=L, shared_init=shared),
        grid=(B // nb, nc),
        in_specs=[
            seq,
            pl.BlockSpec((nb, SUBLANES, D_MODEL), lambda b, c: (b, jnp.maximum(c * (L // SUBLANES) - 1, 0), 0)),
            pl.BlockSpec((nb0, SUBLANES, D_MODEL), st),
            seq,
            pl.BlockSpec((nb, L, LANES), lambda b, c: (b, c, 0)),
            pl.BlockSpec((nb, 1, SUBLANES, L), lambda b, c: (b, c, 0, 0)),
            _full((CONV_W, D_MODEL)), _full((1, D_MODEL)),
            wsq, wsq, wsq, wsq, _full((1, D_MODEL)),
            pl.BlockSpec((nb0, NH_M, DH_M, DH_M), st4),
            pl.BlockSpec((nb0, SUBLANES, DH_M), st),
            pl.BlockSpec((nb0, SUBLANES, LANES), st),
        ],
        out_specs=(
            seq,
            pl.BlockSpec((nb, NH_M, DH_M, DH_M), lambda b, c: (b, 0, 0, 0)),
            pl.BlockSpec((nb, SUBLANES, DH_M), lambda b, c: (b, 0, 0)),
            pl.BlockSpec((nb, SUBLANES, LANES), lambda b, c: (b, 0, 0)),
        ),
        out_shape=(
            jax.ShapeDtypeStruct((B, T, D_MODEL), BF16),
            jax.ShapeDtypeStruct((B, NH_M, DH_M, DH_M), F32),
            jax.ShapeDtypeStruct((B, SUBLANES, DH_M), F32),
            jax.ShapeDtypeStruct((B, SUBLANES, LANES), F32),
        ),
        scratch_shapes=[pltpu.VMEM((nb, SUBLANES + L, D_MODEL), F32)],
        compiler_params=_cparams(("parallel", "arbitrary")),
        name="mlstm",
    )(xm, xm, buf0, om, gcol, grow, w["conv_w"], w["conv_b"], w["mq"], w["mk"], w["mv"], w["mvt"],
      w["mnorm"], c0, n0, m0)


def _lam(lq1_ref, lk1_ref, lq2_ref, lk2_ref):
    return (jnp.exp(jnp.sum(lq1_ref[...] * lk1_ref[...], axis=1, keepdims=True))
            - jnp.exp(jnp.sum(lq2_ref[...] * lk2_ref[...], axis=1, keepdims=True)) + LAMBDA_INIT)


def _online_softmax_step(m_ref, l_ref, acc_ref, s, vb):
    m_old = m_ref[...]
    m_new = jnp.maximum(m_old, jnp.max(s, axis=1, keepdims=True))
    pr = jnp.exp2(s - m_new)
    r = jnp.exp2(m_old - m_new)
    l_ref[...] = r * l_ref[...] + jnp.sum(pr, axis=1, keepdims=True)
    acc_ref[...] = r * acc_ref[...] + _dot(pr.astype(BF16), vb)
    m_ref[...] = m_new


def _attn_prompt_body(q_ref, k_ref, v_ref, kp_ref, vpt_ref, lq1_ref, lk1_ref, lq2_ref, lk2_ref, subln_ref,
                      o_ref, kb_ref, vt_ref, m_ref, l_ref, acc_ref, sa_ref, sb_ref, *, tq, has_prefix):
    qi = pl.program_id(2)
    nblk = vt_ref.shape[0]

    @pl.when(qi == 0)
    def _():
        kb_ref[...] = k_ref[0].astype(BF16)
        for c in range(nblk):
            vt_ref[c] = v_ref[0, c * tq:(c + 1) * tq, :].T.astype(BF16)

    q = q_ref[0].astype(F32)
    lane = lax.broadcasted_iota(jnp.int32, q.shape, 1)
    qq = jnp.concatenate([jnp.where(lane < DH_D, q, 0.0), jnp.where(lane < DH_D, 0.0, q)], axis=0)
    qqt = qq.T.astype(BF16)
    m_ref[...] = jnp.full(m_ref.shape, -jnp.inf, F32)
    l_ref[...] = jnp.zeros(l_ref.shape, F32)
    acc_ref[...] = jnp.zeros(acc_ref.shape, F32)

    def step(st, vt):
        m_old = m_ref[...]
        m_new = jnp.maximum(m_old, jnp.max(st, axis=0, keepdims=True))
        pr = jnp.exp2(st - m_new)
        r = jnp.exp2(m_old - m_new)
        l_ref[...] = r * l_ref[...] + jnp.sum(pr, axis=0, keepdims=True)
        acc_ref[...] = r * acc_ref[...] + _dot(vt, pr.astype(BF16))
        m_ref[...] = m_new

    krel = lax.broadcasted_iota(jnp.int32, (tq, 2 * tq), 0)
    qrel = lax.broadcasted_iota(jnp.int32, (tq, 2 * tq), 1) % tq
    causal = krel <= qrel

    def scores(j):
        st = _dot(kb_ref[pl.ds(pl.multiple_of(j * tq, tq), tq), :], qqt)
        return jnp.where(jnp.logical_or(causal, j < qi), st, -jnp.inf)

    if has_prefix:
        step(_dot(kp_ref[...].astype(BF16), qqt), vpt_ref[...].astype(BF16))

    sa_ref[...] = scores(0)

    def pair(jj, carry):
        j = 2 * jj
        sb_ref[...] = scores(j + 1)
        step(sa_ref[...], vt_ref[j])
        sa_ref[...] = scores(j + 2)
        step(sb_ref[...], vt_ref[j + 1])
        return carry

    lax.fori_loop(0, qi // 2, pair, 0)
    last_even = 2 * (qi // 2)
    step(sa_ref[...], vt_ref[last_even])

    @pl.when(qi % 2 == 1)
    def _():
        step(scores(qi), vt_ref[qi])

    lam = _lam(lq1_ref, lk1_ref, lq2_ref, lk2_ref)
    ot = acc_ref[:, 0:tq] / l_ref[:, 0:tq] - lam * (acc_ref[:, tq:2 * tq] / l_ref[:, tq:2 * tq])
    ot = ot * lax.rsqrt(jnp.mean(ot * ot, axis=0, keepdims=True) + EPS) * subln_ref[...]
    o_ref[0] = (ot * (1.0 - LAMBDA_INIT)).T.astype(BF16)


def _attn_prompt(q, k, v, kpre, vpre_t, lams, subln_col, *, tq, has_prefix):
    B, T, _ = q.shape
    P = kpre.shape[0]
    lspec = _full((1, DH_D))
    return pl.pallas_call(
        functools.partial(_attn_prompt_body, tq=tq, has_prefix=has_prefix),
        grid=(B, NH_D, T // tq),
        in_specs=[
            pl.BlockSpec((1, tq, LANES), lambda b, h, i: (b, i, h)),
            pl.BlockSpec((1, T, LANES), lambda b, h, i: (b, 0, h)),
            pl.BlockSpec((1, T, LANES), lambda b, h, i: (b, 0, h)),
            pl.BlockSpec((P, LANES), lambda b, h, i: (0, h)),
            pl.BlockSpec((DV_D, P), lambda b, h, i: (h, 0)),
            lspec, lspec, lspec, lspec, _full((DV_D, 1)),
        ],
        out_specs=pl.BlockSpec((1, tq, LANES), lambda b, h, i: (b, i, h)),
        out_shape=jax.ShapeDtypeStruct((B, T, D_MODEL), BF16),
        scratch_shapes=[pltpu.VMEM((T, LANES), BF16), pltpu.VMEM((T // tq, DV_D, tq), BF16),
                        pltpu.VMEM((1, 2 * tq), F32), pltpu.VMEM((1, 2 * tq), F32),
                        pltpu.VMEM((DV_D, 2 * tq), F32),
                        pltpu.VMEM((tq, 2 * tq), F32), pltpu.VMEM((tq, 2 * tq), F32)],
        compiler_params=_cparams(("parallel", "parallel", "arbitrary")),
        name="attn_prompt",
    )(q, k, v, kpre, vpre_t, *lams, subln_col)


QROWS = 8
QSUB = 4
QMAT_ROWS = 2 * NH_D * QSUB
PAGES_PER_STEP = 4


def _attn_sample_body(pt_ref, q_ref, *refs, n_new, n_pg):
    kc_refs, vc_refs = refs[:n_pg], refs[n_pg:2 * n_pg]
    (kn_ref, vn_ref, lq1_ref, lk1_ref, lq2_ref, lk2_ref, subln_ref,
     o_ref, qm_ref, bias_ref, m_ref, l_ref, acc_ref) = refs[2 * n_pg:]
    p = pl.program_id(1)
    R = QMAT_ROWS
    rows = kc_refs[0].shape[1]
    half = R // 2

    def head_of(r):
        return (r % half) // QSUB

    @pl.when(p == 0)
    def _():
        q = q_ref[0].astype(F32)
        sub = lax.broadcasted_iota(jnp.int32, (QROWS, LANES), 0)
        pieces = []
        for j in range(NH_D // 2):
            even = q[:, (2 * j) * LANES:(2 * j + 1) * LANES]
            odd = q[:, (2 * j + 1) * LANES:(2 * j + 2) * LANES]
            pieces.append(jnp.where(sub < QSUB, even, pltpu.roll(odd, QSUB, axis=0)))
        qh = jnp.concatenate(pieces, axis=0)
        lane = lax.broadcasted_iota(jnp.int32, qh.shape, 1)
        qm_ref[...] = jnp.concatenate([jnp.where(lane < DH_D, qh, 0.0), jnp.where(lane < DH_D, 0.0, qh)],
                                      axis=0).astype(BF16)
        row = lax.broadcasted_iota(jnp.int32, (R, rows), 0)
        col = lax.broadcasted_iota(jnp.int32, (R, rows), 1)
        bias_ref[...] = jnp.where(col % NH_D == head_of(row), 0.0, -jnp.inf)
        m_ref[...] = jnp.full(m_ref.shape, -jnp.inf, F32)
        l_ref[...] = jnp.zeros(l_ref.shape, F32)
        acc_ref[...] = jnp.zeros(acc_ref.shape, F32)

    qm = qm_ref[...]
    bias = bias_ref[...]
    s = [_dot_nt(qm, kc[0].astype(BF16)) + bias for kc in kc_refs]
    m_old = m_ref[...]
    m_new = m_old
    for si in s:
        m_new = jnp.maximum(m_new, jnp.max(si, axis=1, keepdims=True))
    r = jnp.exp2(m_old - m_new)
    l_new = r * l_ref[...]
    acc = r * acc_ref[...]
    for si, vc in zip(s, vc_refs):
        pr = jnp.exp2(si - m_new)
        l_new = l_new + jnp.sum(pr, axis=1, keepdims=True)
        acc = acc + _dot(pr.astype(BF16), vc[0].astype(BF16))
    m_ref[...] = m_new
    l_ref[...] = l_new
    acc_ref[...] = acc

    @pl.when(p == pl.num_programs(1) - 1)
    def _():
        zpad = jnp.zeros((LANES - kn_ref.shape[1], LANES), F32)
        kn = jnp.concatenate([kn_ref[0], zpad], axis=0).astype(BF16)
        vn = jnp.concatenate([vn_ref[0], zpad], axis=0).astype(BF16)
        sn = _dot_nt(qm, kn)
        row = lax.broadcasted_iota(jnp.int32, sn.shape, 0)
        col = lax.broadcasted_iota(jnp.int32, sn.shape, 1)
        tok = col // NH_D
        ok = (col % NH_D == head_of(row)) & (tok <= row % QSUB) & (tok < n_new)
        _online_softmax_step(m_ref, l_ref, acc_ref, jnp.where(ok, sn, -jnp.inf), vn)
        lam = _lam(lq1_ref, lk1_ref, lq2_ref, lk2_ref)
        o = acc_ref[0:half, :] / l_ref[0:half, :] - lam * (acc_ref[half:R, :] / l_ref[half:R, :])
        o_ref[0] = (_rms(o, subln_ref[...]) * (1.0 - LAMBDA_INIT)).astype(BF16)


def _attn_sample(page_table, q, cache_k, cache_v, knew, vnew, lams, subln, *, n_new):
    B, n_pages = page_table.shape
    rows = cache_k.shape[1]
    R = QMAT_ROWS
    n_pg = math.gcd(n_pages, PAGES_PER_STEP)
    lspec = pl.BlockSpec((1, DH_D), lambda b, p, pt: (0, 0))
    pgs = [pl.BlockSpec((1, rows, LANES), lambda b, p, pt, i=i: (pt[b, p * n_pg + i], 0, 0)) for i in range(n_pg)]
    new = pl.BlockSpec((1, n_new * NH_D, LANES), lambda b, p, pt: (b, 0, 0))
    return pl.pallas_call(
        functools.partial(_attn_sample_body, n_new=n_new, n_pg=n_pg),
        grid_spec=pltpu.PrefetchScalarGridSpec(
            num_scalar_prefetch=1,
            grid=(B, n_pages // n_pg),
            in_specs=[pl.BlockSpec((1, QROWS, D_MODEL), lambda b, p, pt: (b, 0, 0))] + pgs + pgs
                     + [new, new, lspec, lspec, lspec, lspec, pl.BlockSpec((1, DV_D), lambda b, p, pt: (0, 0))],
            out_specs=pl.BlockSpec((1, R // 2, DV_D), lambda b, p, pt: (b, 0, 0)),
            scratch_shapes=[pltpu.VMEM((R, LANES), BF16), pltpu.VMEM((R, rows), F32), pltpu.VMEM((R, 1), F32),
                            pltpu.VMEM((R, 1), F32), pltpu.VMEM((R, DV_D), F32)],
        ),
        out_shape=jax.ShapeDtypeStruct((B, R // 2, DV_D), BF16),
        compiler_params=_cparams(("parallel", "arbitrary")),
        name="attn_sample",
    )(page_table, q, *([cache_k] * n_pg), *([cache_v] * n_pg), knew, vnew, *lams, subln)


def _merge_body(hm_ref, o_ref, ga_ref, gb_ref, x_ref, wa_ref, wb_ref, wo_ref, y_ref):
    ya = _dot(hm_ref[...], wa_ref[...])
    yb = _dot(o_ref[...], wb_ref[...])
    merged = _sigmoid(ga_ref[...]) * ya + _sigmoid(gb_ref[...]) * yb
    y_ref[...] = x_ref[...] + _dot(merged.astype(BF16), wo_ref[...])


def _merge(hm, o, ga, gb, x, wa, wb, wo, *, tm=512):
    n = x.shape[0]
    tm = _row_tile(n, tm)
    row = pl.BlockSpec((tm, D_MODEL), lambda i: (i, 0))
    wspec = _full((D_MODEL, D_MODEL))
    return pl.pallas_call(
        _merge_body,
        grid=(n // tm,),
        in_specs=[row, row, row, row, row, wspec, wspec, wspec],
        out_specs=row,
        out_shape=jax.ShapeDtypeStruct((n, D_MODEL), F32),
        compiler_params=_cparams(("parallel",)),
        name="merge",
    )(hm, o, ga, gb, x, wa, wb, wo)


def _rope_tables(pos):
    half = ROT_DIM // 2
    inv_freq = ROPE_THETA ** (-jnp.arange(half, dtype=F32) / half)
    ang = pos.astype(F32)[:, None] * inv_freq[None, :]
    cos, sin = jnp.cos(ang), jnp.sin(ang)
    n = pos.shape[0]
    ones = jnp.ones((n, DH_D - ROT_DIM), F32)
    zeros = jnp.zeros((n, DH_D - ROT_DIM), F32)
    zh = jnp.zeros((n, half), F32)
    c = jnp.concatenate([cos, cos, ones] * 2, axis=1)
    s_up = jnp.concatenate([-sin, zh, zeros] * 2, axis=1)
    s_dn = jnp.concatenate([zh, sin, zeros] * 2, axis=1)
    return c, s_up, s_dn


def _pad_rows(a, rows, axis, front=False):
    pad = [(0, 0)] * a.ndim
    pad[axis] = (rows - a.shape[axis], 0) if front else (0, rows - a.shape[axis])
    return jnp.pad(a, pad)


def kernel(x_prompt, x_sample, cache_k, cache_v, page_table, state_C, state_n, state_m, state_conv, meta_tokens, ffn1_norm, ffn1_w_gate, ffn1_w_up, ffn1_w_down, mix_norm, w_in, conv_w, conv_b, w_mq, w_mk, w_mv, b_igate, b_fgate, mlstm_norm, lambda_q1, lambda_k1, lambda_q2, lambda_k2, subln, w_proj_a, w_proj_b, w_out, ffn2_norm, ffn2_w_gate, ffn2_w_up, ffn2_w_down, final_norm):
    bp, seq, _ = x_prompt.shape
    bs, ts, _ = x_sample.shape
    n_pages = page_table.shape[1]
    page = cache_k.shape[2]
    past_len = n_pages * page
    assert w_in.shape[0] == 1 and ts <= QSUB

    win = w_in[0]
    offs = [0]
    for width in (D_MODEL, D_MODEL, NH_M, NH_M, D_MODEL, D_MODEL, D_MODEL, D_MODEL, D_MODEL):
        offs.append(offs[-1] + width)
    col = lambda i: win[:, offs[i]:offs[i + 1]]
    wgate = jnp.pad(jnp.concatenate([col(2), col(3)], axis=1), ((0, 0), (0, LANES - 2 * NH_M)))
    wproj = {"xm": col(0).astype(BF16), "om": col(1).astype(BF16), "q": col(4).astype(BF16),
             "k": col(5).astype(BF16), "v": col(6).astype(BF16), "ga": col(7).astype(BF16),
             "gb": col(8).astype(BF16), "gate": wgate.astype(BF16)}
    gbias = jnp.pad(jnp.concatenate([b_igate[0], b_fgate[0]])[None, :], ((0, 0), (0, LANES - 2 * NH_M)))
    wm = {"conv_w": conv_w[0], "conv_b": conv_b[0][None, :],
          "mq": w_mq[0].astype(BF16), "mk": (w_mk[0] * DH_M ** -0.5).astype(BF16), "mv": w_mv[0].astype(BF16),
          "mvt": jnp.swapaxes(w_mv[0], 1, 2).astype(BF16), "mnorm": mlstm_norm[0].reshape(1, D_MODEL)}
    lams = (lambda_q1, lambda_k1, lambda_q2, lambda_k2)
    row = lambda a: a.reshape(1, -1)
    f1 = (row(ffn1_norm[0]), ffn1_w_gate[0].astype(BF16), ffn1_w_up[0].astype(BF16), ffn1_w_down[0].astype(BF16))
    f2 = (row(ffn2_norm[0]), ffn2_w_gate[0].astype(BF16), ffn2_w_up[0].astype(BF16), ffn2_w_down[0].astype(BF16))
    fin = row(final_norm)
    wa, wb, wo = w_proj_a[0].astype(BF16), w_proj_b[0].astype(BF16), w_out[0].astype(BF16)

    n_main = bp * seq
    n_samp = bs * ts
    x_main = x_prompt.reshape(n_main, D_MODEL)
    x_small = jnp.concatenate([x_sample.reshape(n_samp, D_MODEL), meta_tokens], axis=0)

    tm_in = _row_tile(seq, 256)
    tabs_main = _rope_tables(N_META + jnp.arange(seq, dtype=jnp.int32))
    pos_small = jnp.concatenate([jnp.tile(past_len + jnp.arange(ts, dtype=jnp.int32), bs),
                                 jnp.arange(N_META, dtype=jnp.int32)])
    tabs_small = _rope_tables(pos_small)

    x1_main = _ffn(x_main, *f1, fin, final_norm=False)
    x1_small = _ffn(x_small, *f1, fin, final_norm=False)
    pm = _inproj(x1_main, row(mix_norm[0]), wproj, gbias, tabs_main, n_rope_blocks=seq // tm_in, tm=tm_in)
    ps = _inproj(x1_small, row(mix_norm[0]), wproj, gbias, tabs_small, n_rope_blocks=1, tm=x_small.shape[0])
    xm_m, om_m, q_m, k_m, v_m, ga_m, gb_m, gcol_m, grow_m = pm
    xm_s, om_s, q_s, k_s, v_s, ga_s, gb_s, gcol_s, grow_s = ps

    def split(a):
        return a[:n_samp], a[n_samp:]

    xm_smp, xm_meta = split(xm_s)
    om_smp, om_meta = split(om_s)
    gcol_smp, gcol_meta = split(gcol_s)
    grow_smp, grow_meta = grow_s[:, :n_samp], grow_s[:, n_samp:]

    zc = jnp.zeros((1, NH_M, DH_M, DH_M), F32)
    zn = jnp.zeros((1, SUBLANES, DH_M), F32)
    zm = jnp.zeros((1, SUBLANES, LANES), F32)
    zbuf = jnp.zeros((1, SUBLANES, D_MODEL), F32)
    hm_meta, c_meta, n_meta, m_meta = _mlstm(
        xm_meta[None], om_meta[None], gcol_meta[None], grow_meta.reshape(1, 1, SUBLANES, N_META),
        zbuf, zc, zn, zm, wm, chunk=N_META, nb=1)

    chunk = _row_tile(seq, MLSTM_CHUNK)
    grow_main = grow_m.reshape(SUBLANES, bp, seq // chunk, chunk).transpose(1, 2, 0, 3)
    buf_main = _pad_rows(xm_meta[N_META - (CONV_W - 1):], SUBLANES, 0, front=True)[None]
    hm_main, c_main, n_main_st, m_main = _mlstm(
        xm_m.reshape(bp, seq, D_MODEL), om_m.reshape(bp, seq, D_MODEL), gcol_m.reshape(bp, seq, LANES),
        grow_main, buf_main, c_meta, n_meta, m_meta, wm, chunk=chunk, nb=_group(bp))

    pad_t = lambda a: _pad_rows(a.reshape(bs, ts, a.shape[-1]), QROWS, 1)
    gcol_pad = jnp.concatenate(
        [gcol_smp.reshape(bs, ts, LANES),
         jnp.broadcast_to(jnp.where(jnp.arange(LANES) < NH_M, NEG_BIG, 0.0).astype(F32), (bs, QROWS - ts, LANES))],
        axis=1)
    grow_pad = jnp.concatenate(
        [grow_smp.reshape(SUBLANES, bs, ts),
         jnp.broadcast_to(jnp.where(jnp.arange(SUBLANES) < NH_M, NEG_BIG, 0.0).astype(F32)[:, None, None],
                          (SUBLANES, bs, QROWS - ts))], axis=2).transpose(1, 0, 2)[:, None]
    hm_smp, c_smp, n_smp, m_smp = _mlstm(
        pad_t(xm_smp), pad_t(om_smp), gcol_pad, grow_pad,
        _pad_rows(state_conv[0], SUBLANES, 1, front=True), state_C[0],
        _pad_rows(state_n[0], SUBLANES, 1), jnp.broadcast_to(_pad_rows(state_m[0], SUBLANES, 1)[:, :, None],
                                                             (bs, SUBLANES, LANES)),
        wm, chunk=QROWS, nb=_group(bs))

    q_smp, q_meta = split(q_s)
    k_smp, k_meta = split(k_s)
    v_smp, v_meta = split(v_s)
    sub = row(subln[0])
    sub_col = subln[0].reshape(DV_D, 1)
    o_meta = _attn_prompt(q_meta[None], k_meta[None], v_meta[None], k_meta, v_meta.T, lams, sub_col,
                          tq=N_META, has_prefix=False)
    tq = _row_tile(seq, 512)
    o_main = _attn_prompt(q_m.reshape(bp, seq, D_MODEL), k_m.reshape(bp, seq, D_MODEL),
                          v_m.reshape(bp, seq, D_MODEL), k_meta, v_meta.T, lams, sub_col, tq=tq, has_prefix=True)
    del o_meta
    o_smp = _attn_sample(page_table, pad_t(q_smp), cache_k.reshape(-1, page * NH_D, DV_D),
                         cache_v.reshape(-1, page * NH_D, DV_D), k_smp.reshape(bs, ts * NH_D, DV_D),
                         v_smp.reshape(bs, ts * NH_D, DV_D), lams, sub, n_new=ts)
    o_smp = o_smp.reshape(bs, NH_D, QSUB, DV_D)[:, :, :ts].transpose(0, 2, 1, 3).reshape(n_samp, D_MODEL)

    x2_main = _merge(hm_main.reshape(n_main, D_MODEL), o_main.reshape(n_main, D_MODEL), ga_m, gb_m, x1_main,
                     wa, wb, wo)
    y_main = _ffn(x2_main, *f2, fin, final_norm=True)
    x2_smp = _merge(hm_smp[:, :ts].reshape(n_samp, D_MODEL), o_smp,
                    ga_s[:n_samp], gb_s[:n_samp], x1_small[:n_samp], wa, wb, wo)
    y_smp = _ffn(x2_smp, *f2, fin, final_norm=True)
    del hm_meta

    def with_meta(main, meta):
        full = jnp.concatenate([jnp.broadcast_to(meta[None], (bp, N_META, D_MODEL)),
                                main.reshape(bp, seq, D_MODEL)], axis=1)
        return full.reshape(1, bp, seq + N_META, NH_D, DV_D)

    xm_main3 = xm_m.reshape(bp, seq, D_MODEL)
    return (
        y_main.reshape(bp, seq, D_MODEL),
        y_smp.reshape(bs, ts, D_MODEL),
        with_meta(k_m, k_meta),
        with_meta(v_m, v_meta),
        c_main[None],
        n_main_st[None, :, :NH_M, :],
        m_main[None, :, :NH_M, 0],
        xm_main3[None, :, seq - (CONV_W - 1):, :],
        k_smp.reshape(1, bs, ts, NH_D, DV_D),
        v_smp.reshape(1, bs, ts, NH_D, DV_D),
        c_smp[None],
        n_smp[None, :, :NH_M, :],
        m_smp[None, :, :NH_M, 0],
        xm_smp.reshape(bs, ts, D_MODEL)[None, :, ts - (CONV_W - 1):, :],
    )
```

```python
import functools
import math

import jax
import jax.numpy as jnp
from jax import lax
from jax.experimental import pallas as pl
from jax.experimental.pallas import tpu as pltpu

F32 = jnp.float32
BF16 = jnp.bfloat16

D_MODEL = 1024
N_META = 16
EPS = 1e-6
D_FF = 2816
NH_M = 4
DH_M = 256
CONV_W = 4
NH_D = 8
DH_D = 64
DV_D = 128
ROT_DIM = 16
ROPE_THETA = 500000.0
LAMBDA_INIT = 0.8 - 0.6 * math.exp(-0.3 * 0)

LANES = 128
SUBLANES = 8
NEG_BIG = -1e30
VMEM_LIMIT = 56 * 1024 * 1024


def _cparams(sem):
    return pltpu.CompilerParams(dimension_semantics=sem, vmem_limit_bytes=VMEM_LIMIT)


def _rms(x, g):
    return x * lax.rsqrt(jnp.mean(x * x, axis=-1, keepdims=True) + EPS) * g


def _sigmoid(x):
    return 1.0 / (1.0 + jnp.exp(-x))


def _dot(a, b):
    return jnp.dot(a, b, preferred_element_type=F32)


def _dot_nt(a, b):
    return lax.dot_general(a, b, (((1,), (1,)), ((), ())), preferred_element_type=F32)


def _row_tile(n, want):
    if n <= 2 * want:
        return n
    t = want
    while n % t:
        t //= 2
    assert t == n or t % SUBLANES == 0
    return t


MLSTM_CHUNK = 256
MLSTM_GROUP = 4


def _group(n):
    return max(g for g in range(1, MLSTM_GROUP + 1) if n % g == 0)


def _full(shape):
    zeros = (0,) * len(shape)
    return pl.BlockSpec(shape, lambda *_: zeros)


def _ffn_body(x_ref, g_ref, wg_ref, wu_ref, wd_ref, fg_ref, o_ref, *, final_norm):
    x = x_ref[...]
    xb = _rms(x, g_ref[...]).astype(BF16)
    g = _dot(xb, wg_ref[...])
    u = _dot(xb, wu_ref[...])
    a = (g * _sigmoid(g) * u).astype(BF16)
    y = x + 0.5 * _dot(a, wd_ref[...])
    if final_norm:
        y = _rms(y, fg_ref[...])
    o_ref[...] = y


def _ffn(x, gain, wg, wu, wd, final_gain, *, final_norm, tm=512):
    n = x.shape[0]
    tm = _row_tile(n, tm)
    row = pl.BlockSpec((tm, D_MODEL), lambda i: (i, 0))
    return pl.pallas_call(
        functools.partial(_ffn_body, final_norm=final_norm),
        grid=(n // tm,),
        in_specs=[row, _full((1, D_MODEL)), _full(wg.shape), _full(wu.shape), _full(wd.shape),
                  _full((1, D_MODEL))],
        out_specs=row,
        out_shape=jax.ShapeDtypeStruct((n, D_MODEL), F32),
        compiler_params=_cparams(("parallel",)),
        name="ffn_final" if final_norm else "ffn",
    )(x, gain, wg, wu, wd, final_gain)


def _rope_slab(xs, c, s_up, s_dn):
    return xs * c + pltpu.roll(xs, LANES - ROT_DIM // 2, axis=1) * s_up + pltpu.roll(xs, ROT_DIM // 2, axis=1) * s_dn


def _inproj_body(x_ref, g_ref, wxm_ref, wom_ref, wq_ref, wk_ref, wv_ref, wga_ref, wgb_ref, wgate_ref,
                 gbias_ref, cq_ref, cu_ref, cd_ref,
                 xm_ref, om_ref, q_ref, k_ref, v_ref, ga_ref, gb_ref, gcol_ref, grow_ref):
    hb = _rms(x_ref[...], g_ref[...]).astype(BF16)
    xm_ref[...] = _dot(hb, wxm_ref[...])
    om_ref[...] = _dot(hb, wom_ref[...])
    v_ref[...] = _dot(hb, wv_ref[...])
    ga_ref[...] = _dot(hb, wga_ref[...])
    gb_ref[...] = _dot(hb, wgb_ref[...])
    c, s_up, s_dn = cq_ref[...], cu_ref[...], cd_ref[...]
    q = _dot(hb, wq_ref[...])
    k = _dot(hb, wk_ref[...])
    qscale = DH_D ** -0.5 * math.log2(math.e)
    for h in range(NH_D):
        sl = slice(h * LANES, (h + 1) * LANES)
        q_ref[:, sl] = (_rope_slab(q[:, sl], c, s_up, s_dn) * qscale).astype(BF16)
        k_ref[:, sl] = _rope_slab(k[:, sl], c, s_up, s_dn)
    pre = _dot(hb, wgate_ref[...]) + gbias_ref[...]
    lane = lax.broadcasted_iota(jnp.int32, pre.shape, 1)
    logsig = jnp.minimum(pre, 0.0) - jnp.log(1.0 + jnp.exp(-jnp.abs(pre)))
    glog = jnp.where(lane < NH_M, pre, logsig)
    gcol_ref[...] = glog
    grow_ref[...] = glog.T[:SUBLANES, :]


def _inproj(x, gain, w, gbias, rope_tabs, *, n_rope_blocks, tm):
    n = x.shape[0]
    assert n % tm == 0
    row = pl.BlockSpec((tm, D_MODEL), lambda i: (i, 0))
    tab = pl.BlockSpec((tm, LANES), lambda i: (i % n_rope_blocks, 0))
    wspec = _full((D_MODEL, D_MODEL))
    out_shapes = (
        jax.ShapeDtypeStruct((n, D_MODEL), F32),
        jax.ShapeDtypeStruct((n, D_MODEL), F32),
        jax.ShapeDtypeStruct((n, D_MODEL), BF16),
        jax.ShapeDtypeStruct((n, D_MODEL), F32),
        jax.ShapeDtypeStruct((n, D_MODEL), F32),
        jax.ShapeDtypeStruct((n, D_MODEL), F32),
        jax.ShapeDtypeStruct((n, D_MODEL), F32),
        jax.ShapeDtypeStruct((n, LANES), F32),
        jax.ShapeDtypeStruct((SUBLANES, n), F32),
    )
    return pl.pallas_call(
        _inproj_body,
        grid=(n // tm,),
        in_specs=[row, _full((1, D_MODEL))] + [wspec] * 7 + [_full((D_MODEL, LANES)), _full((1, LANES)),
                                                              tab, tab, tab],
        out_specs=(row, row, row, row, row, row, row,
                   pl.BlockSpec((tm, LANES), lambda i: (i, 0)),
                   pl.BlockSpec((SUBLANES, tm), lambda i: (0, i))),
        out_shape=out_shapes,
        compiler_params=_cparams(("parallel",)),
        name="inproj",
    )(x, gain, w["xm"], w["om"], w["q"], w["k"], w["v"], w["ga"], w["gb"], w["gate"], gbias, *rope_tabs)


def _mlstm_body(xm_ref, xprev_ref, buf0_ref, om_ref, gcol_ref, grow_ref, cw_ref, cb_ref,
                wq_ref, wk_ref, wv_ref, wvt_ref, norm_ref, c0_ref, n0_ref, m0_ref,
                hm_ref, c_ref, n_ref, m_ref, ext_ref, *, chunk, shared_init):
    ci = pl.program_id(1)
    L = chunk
    nb = xm_ref.shape[0]

    @pl.when(ci == 0)
    def _():
        for bi in range(nb):
            src = 0 if shared_init else bi
            c_ref[bi] = c0_ref[src]
            n_ref[bi] = n0_ref[src]
            m_ref[bi] = m0_ref[src]
            ext_ref[bi, 0:SUBLANES, :] = buf0_ref[src]

    @pl.when(ci > 0)
    def _():
        ext_ref[:, 0:SUBLANES, :] = xprev_ref[...]

    x = xm_ref[...]
    ext_ref[:, SUBLANES:SUBLANES + L, :] = x
    cv = cb_ref[...]
    for j in range(CONV_W):
        cv = cv + ext_ref[:, pl.ds(SUBLANES - (CONV_W - 1) + j, L), :] * cw_ref[j:j + 1, :]
    cact = cv * _sigmoid(cv)

    ti = lax.broadcasted_iota(jnp.int32, (L, L), 0)
    si = lax.broadcasted_iota(jnp.int32, (L, L), 1)
    tril = si <= ti

    for h in range(NH_M):
        hs = slice(h * DH_M, (h + 1) * DH_M)
        chb = cact[:, :, hs].reshape(nb * L, DH_M).astype(BF16)
        xhb = x[:, :, hs].reshape(nb * L, DH_M).astype(BF16)
        q_all = _dot(chb, wq_ref[h])
        k_all = _dot(chb, wk_ref[h])
        v_all = _dot(xhb, wv_ref[h])
        for bi in range(nb):
            rows = slice(bi * L, (bi + 1) * L)
            q, k, v = q_all[rows], k_all[rows], v_all[rows]
            vt = _dot_nt(wvt_ref[h], xhb[rows])
            qb, kb, vb = q.astype(BF16), k.astype(BF16), v.astype(BF16)

            gcol = gcol_ref[bi]
            grow = grow_ref[bi, 0]
            logi_c, logf_c = gcol[:, h:h + 1], gcol[:, NH_M + h:NH_M + h + 1]
            logi_r, logf_r = grow[h:h + 1, :], grow[NH_M + h:NH_M + h + 1, :]
            b_c = jnp.sum(jnp.where(tril, logf_r, 0.0), axis=1, keepdims=True)
            b_r = jnp.sum(jnp.where(ti <= si, logf_c, 0.0), axis=0, keepdims=True)
            m_prev = m_ref[bi, h:h + 1, 0:1]
            n_prev = n_ref[bi, h:h + 1, :]
            c_prev = c_ref[bi, h]

            d = jnp.where(tril, b_c - b_r + logi_r, -jnp.inf)
            inter = b_c + m_prev
            m_t = jnp.maximum(jnp.max(d, axis=1, keepdims=True), inter)
            w_intra = jnp.exp(d - m_t)
            w_inter = jnp.exp(inter - m_t)
            s = _dot_nt(qb, kb) * w_intra
            num = _dot(s.astype(BF16), vb) + w_inter * _dot_nt(qb, c_prev.astype(BF16))
            den = jnp.sum(s, axis=1, keepdims=True) + w_inter * jnp.sum(q * n_prev, axis=1, keepdims=True)
            hh = num / jnp.maximum(jnp.abs(den), jnp.exp(-m_t))

            m_new = m_t[L - 1:L, :]
            b_last = b_c[L - 1:L, :]
            w_state_r = jnp.exp(b_last - b_r + logi_r - m_new)
            w_state_c = jnp.exp(b_last - b_c + logi_c - m_new)
            decay = jnp.exp(b_last + m_prev - m_new)
            c_ref[bi, h] = decay * c_prev + _dot((vt * w_state_r).astype(BF16), kb)
            n_ref[bi, h:h + 1, :] = decay * n_prev + jnp.sum(k * w_state_c, axis=0, keepdims=True)
            m_ref[bi, h:h + 1, :] = jnp.broadcast_to(m_new, (1, LANES))

            hn = hh * lax.rsqrt(jnp.mean(hh * hh, axis=1, keepdims=True) + EPS) * norm_ref[:, hs]
            hm_ref[bi, :, hs] = (hn * _sigmoid(om_ref[bi, :, hs])).astype(BF16)


def _mlstm(xm, om, gcol, grow, buf0, c0, n0, m0, w, *, chunk, nb):
    B, T, _ = xm.shape
    L = chunk
    nc = T // L
    shared = c0.shape[0] == 1 and B > 1
    nb0 = 1 if shared else nb
    st = (lambda b, c: (0, 0, 0)) if shared else (lambda b, c: (b, 0, 0))
    st4 = (lambda b, c: (0, 0, 0, 0)) if shared else (lambda b, c: (b, 0, 0, 0))
    seq = pl.BlockSpec((nb, L, D_MODEL), lambda b, c: (b, c, 0))
    wsq = _full((NH_M, DH_M, DH_M))
    return pl.pallas_call(
        functools.partial(_mlstm_body, chunk=L, shared_init=shared),
        grid=(B // nb, nc),
        in_specs=[
            seq,
            pl.BlockSpec((nb, SUBLANES, D_MODEL), lambda b, c: (b, jnp.maximum(c * (L // SUBLANES) - 1, 0), 0)),
            pl.BlockSpec((nb0, SUBLANES, D_MODEL), st),
            seq,
            pl.BlockSpec((nb, L, LANES), lambda b, c: (b, c, 0)),
            pl.BlockSpec((nb, 1, SUBLANES, L), lambda b, c: (b, c, 0, 0)),
            _full((CONV_W, D_MODEL)), _full((1, D_MODEL)),
            wsq, wsq, wsq, wsq, _full((1, D_MODEL)),
            pl.BlockSpec((nb0, NH_M, DH_M, DH_M), st4),
            pl.BlockSpec((nb0, SUBLANES, DH_M), st),
            pl.BlockSpec((nb0, SUBLANES, LANES), st),
        ],
        out_specs=(
            seq,
            pl.BlockSpec((nb, NH_M, DH_M, DH_M), lambda b, c: (b, 0, 0, 0)),
            pl.BlockSpec((nb, SUBLANES, DH_M), lambda b, c: (b, 0, 0)),
            pl.BlockSpec((nb, SUBLANES, LANES), lambda b, c: (b, 0, 0)),
        ),
        out_shape=(
            jax.ShapeDtypeStruct((B, T, D_MODEL), BF16),
            jax.ShapeDtypeStruct((B, NH_M, DH_M, DH_M), F32),
            jax.ShapeDtypeStruct((B, SUBLANES, DH_M), F32),
            jax.ShapeDtypeStruct((B, SUBLANES, LANES), F32),
        ),
        scratch_shapes=[pltpu.VMEM((nb, SUBLANES + L, D_MODEL), F32)],
        compiler_params=_cparams(("parallel", "arbitrary")),
        name="mlstm",
    )(xm, xm, buf0, om, gcol, grow, w["conv_w"], w["conv_b"], w["mq"], w["mk"], w["mv"], w["mvt"],
      w["mnorm"], c0, n0, m0)


def _lam(lq1_ref, lk1_ref, lq2_ref, lk2_ref):
    return (jnp.exp(jnp.sum(lq1_ref[...] * lk1_ref[...], axis=1, keepdims=True))
            - jnp.exp(jnp.sum(lq2_ref[...] * lk2_ref[...], axis=1, keepdims=True)) + LAMBDA_INIT)


def _online_softmax_step(m_ref, l_ref, acc_ref, s, vb):
    m_old = m_ref[...]
    m_new = jnp.maximum(m_old, jnp.max(s, axis=1, keepdims=True))
    pr = jnp.exp2(s - m_new)
    r = jnp.exp2(m_old - m_new)
    l_ref[...] = r * l_ref[...] + jnp.sum(pr, axis=1, keepdims=True)
    acc_ref[...] = r * acc_ref[...] + _dot(pr.astype(BF16), vb)
    m_ref[...] = m_new


ATTN_TQ = 1024
ONES_ROWS = 16


def _attn_prompt_body(q_ref, k_ref, v_ref, kp_ref, vp_ref, vpt_ref, lq1_ref, lk1_ref, lq2_ref, lk2_ref, subln_ref,
                      o_ref, kf_ref, vf_ref, kb_ref, vt_ref, m_ref, acc_ref, sa_ref, sb_ref, ma_ref, mb_ref, *, tq, tk,
                      has_prefix):
    qi = pl.program_id(2)
    nblk = vt_ref.shape[0]

    def with_ones(vt):
        keys = vt.shape[1]
        row = lax.broadcasted_iota(jnp.int32, (ONES_ROWS, keys), 0)
        return jnp.concatenate([vt, jnp.where(row == 0, 1.0, 0.0)], axis=0).astype(BF16)

    @pl.when(qi == 0)
    def _():
        npre = kp_ref.shape[0]
        kf_ref[0, 0:npre, :] = kp_ref[...]
        vf_ref[0, 0:npre, :] = vp_ref[...]
        kf_ref[0, npre:, :] = k_ref[0]
        vf_ref[0, npre:, :] = v_ref[0]
        kb_ref[...] = k_ref[0].astype(BF16)
        for c in range(nblk):
            vt_ref[c] = with_ones(v_ref[0, c * tk:(c + 1) * tk, :].T)

    q = q_ref[0].astype(F32)
    lane = lax.broadcasted_iota(jnp.int32, q.shape, 1)
    qq = jnp.concatenate([jnp.where(lane < DH_D, q, 0.0), jnp.where(lane < DH_D, 0.0, q)], axis=0)
    qqt = qq.T.astype(BF16)
    m_ref[...] = jnp.full(m_ref.shape, -jnp.inf, F32)
    acc_ref[...] = jnp.zeros(acc_ref.shape, F32)

    def apply(st, mx, vt):
        m_old = m_ref[...]
        m_new = jnp.maximum(m_old, mx)
        acc_ref[...] = jnp.exp2(m_old - m_new) * acc_ref[...] + _dot(vt, jnp.exp2(st - m_new).astype(BF16))
        m_ref[...] = m_new

    def step(buf, j):
        s_ref, mx_ref = buf
        apply(s_ref[...], mx_ref[...], vt_ref[j])

    rel = (lax.broadcasted_iota(jnp.int32, (tk, 2 * tq), 0)
           - lax.broadcasted_iota(jnp.int32, (tk, 2 * tq), 1) % tq)

    def put(buf, j, masked):
        s_ref, mx_ref = buf
        st = _dot(kb_ref[pl.ds(pl.multiple_of(j * tk, tk), tk), :], qqt)
        if masked:
            st = jnp.where(rel <= qi * tq - j * tk, st, -jnp.inf)
        s_ref[...] = st
        mx_ref[...] = jnp.max(st, axis=0, keepdims=True)

    if has_prefix:
        st = _dot(kp_ref[...].astype(BF16), qqt)
        apply(st, jnp.max(st, axis=0, keepdims=True), with_ones(vpt_ref[...]))

    assert tq == 2 * tk
    buf_a, buf_b = (sa_ref, ma_ref), (sb_ref, mb_ref)
    put(buf_a, 0, True)

    def pair(p, carry):
        j = 2 * p
        put(buf_b, j + 1, False)
        step(buf_a, j)
        put(buf_a, j + 2, False)
        step(buf_b, j + 1)
        return carry

    lax.fori_loop(0, qi - 1, pair, 0)

    @pl.when(qi >= 1)
    def _():
        j = 2 * (qi - 1)
        put(buf_b, j + 1, False)
        step(buf_a, j)
        put(buf_a, j + 2, True)
        step(buf_b, j + 1)

    put(buf_b, 2 * qi + 1, True)
    step(buf_a, 2 * qi)
    step(buf_b, 2 * qi + 1)

    lam = _lam(lq1_ref, lk1_ref, lq2_ref, lk2_ref)
    acc = acc_ref[0:DV_D, :] / acc_ref[DV_D:DV_D + 1, :]
    ot = acc[:, 0:tq] - lam * acc[:, tq:2 * tq]
    ot = ot * lax.rsqrt(jnp.mean(ot * ot, axis=0, keepdims=True) + EPS) * subln_ref[...]
    o_ref[0] = (ot * (1.0 - LAMBDA_INIT)).T.astype(BF16)


def _attn_prompt(q, k, v, kpre, vpre, lams, subln_col, *, tq, has_prefix):
    vpre_t = vpre.T
    B, T, _ = q.shape
    P = kpre.shape[0]
    tk = tq // 2
    assert T % tq == 0 and tq % (2 * tk) == 0
    lspec = _full((1, DH_D))
    return pl.pallas_call(
        functools.partial(_attn_prompt_body, tq=tq, tk=tk, has_prefix=has_prefix),
        grid=(B, NH_D, T // tq),
        in_specs=[
            pl.BlockSpec((1, tq, LANES), lambda b, h, i: (b, i, h)),
            pl.BlockSpec((1, T, LANES), lambda b, h, i: (b, 0, h)),
            pl.BlockSpec((1, T, LANES), lambda b, h, i: (b, 0, h)),
            pl.BlockSpec((P, LANES), lambda b, h, i: (0, h)),
            pl.BlockSpec((P, LANES), lambda b, h, i: (0, h)),
            pl.BlockSpec((DV_D, P), lambda b, h, i: (h, 0)),
            lspec, lspec, lspec, lspec, _full((DV_D, 1)),
        ],
        out_specs=(pl.BlockSpec((1, tq, LANES), lambda b, h, i: (b, i, h)),
                   pl.BlockSpec((1, P + T, LANES), lambda b, h, i: (b, 0, h)),
                   pl.BlockSpec((1, P + T, LANES), lambda b, h, i: (b, 0, h))),
        out_shape=(jax.ShapeDtypeStruct((B, T, D_MODEL), BF16),
                   jax.ShapeDtypeStruct((B, P + T, D_MODEL), F32),
                   jax.ShapeDtypeStruct((B, P + T, D_MODEL), F32)),
        scratch_shapes=[pltpu.VMEM((T, LANES), BF16), pltpu.VMEM((T // tk, DV_D + ONES_ROWS, tk), BF16),
                        pltpu.VMEM((1, 2 * tq), F32), pltpu.VMEM((DV_D + ONES_ROWS, 2 * tq), F32),
                        pltpu.VMEM((tk, 2 * tq), F32), pltpu.VMEM((tk, 2 * tq), F32),
                        pltpu.VMEM((1, 2 * tq), F32), pltpu.VMEM((1, 2 * tq), F32)],
        compiler_params=_cparams(("parallel", "parallel", "arbitrary")),
        name="attn_prompt",
    )(q, k, v, kpre, vpre, vpre_t, *lams, subln_col)


QROWS = 8
QSUB = 4
QMAT_ROWS = 2 * NH_D * QSUB
PAGES_PER_STEP = 4


def _attn_sample_body(pt_ref, q_ref, *refs, n_new, n_pg):
    kc_refs, vc_refs = refs[:n_pg], refs[n_pg:2 * n_pg]
    (kn_ref, vn_ref, lq1_ref, lk1_ref, lq2_ref, lk2_ref, subln_ref,
     o_ref, qm_ref, bias_ref, m_ref, l_ref, acc_ref) = refs[2 * n_pg:]
    p = pl.program_id(1)
    R = QMAT_ROWS
    rows = kc_refs[0].shape[1]
    half = R // 2

    def head_of(r):
        return (r % half) // QSUB

    @pl.when(p == 0)
    def _():
        q = q_ref[0].astype(F32)
        sub = lax.broadcasted_iota(jnp.int32, (QROWS, LANES), 0)
        pieces = []
        for j in range(NH_D // 2):
            even = q[:, (2 * j) * LANES:(2 * j + 1) * LANES]
            odd = q[:, (2 * j + 1) * LANES:(2 * j + 2) * LANES]
            pieces.append(jnp.where(sub < QSUB, even, pltpu.roll(odd, QSUB, axis=0)))
        qh = jnp.concatenate(pieces, axis=0)
        lane = lax.broadcasted_iota(jnp.int32, qh.shape, 1)
        qm_ref[...] = jnp.concatenate([jnp.where(lane < DH_D, qh, 0.0), jnp.where(lane < DH_D, 0.0, qh)],
                                      axis=0).astype(BF16)
        row = lax.broadcasted_iota(jnp.int32, (R, rows), 0)
        col = lax.broadcasted_iota(jnp.int32, (R, rows), 1)
        bias_ref[...] = jnp.where(col % NH_D == head_of(row), 0.0, -jnp.inf)
        m_ref[...] = jnp.full(m_ref.shape, -jnp.inf, F32)
        l_ref[...] = jnp.zeros(l_ref.shape, F32)
        acc_ref[...] = jnp.zeros(acc_ref.shape, F32)

    qm = qm_ref[...]
    bias = bias_ref[...]
    s = [_dot_nt(qm, kc[0].astype(BF16)) + bias for kc in kc_refs]
    m_old = m_ref[...]
    m_new = m_old
    for si in s:
        m_new = jnp.maximum(m_new, jnp.max(si, axis=1, keepdims=True))
    r = jnp.exp2(m_old - m_new)
    l_new = r * l_ref[...]
    acc = r * acc_ref[...]
    for si, vc in zip(s, vc_refs):
        pr = jnp.exp2(si - m_new)
        l_new = l_new + jnp.sum(pr, axis=1, keepdims=True)
        acc = acc + _dot(pr.astype(BF16), vc[0].astype(BF16))
    m_ref[...] = m_new
    l_ref[...] = l_new
    acc_ref[...] = acc

    @pl.when(p == pl.num_programs(1) - 1)
    def _():
        zpad = jnp.zeros((LANES - kn_ref.shape[1], LANES), F32)
        kn = jnp.concatenate([kn_ref[0], zpad], axis=0).astype(BF16)
        vn = jnp.concatenate([vn_ref[0], zpad], axis=0).astype(BF16)
        sn = _dot_nt(qm, kn)
        row = lax.broadcasted_iota(jnp.int32, sn.shape, 0)
        col = lax.broadcasted_iota(jnp.int32, sn.shape, 1)
        tok = col // NH_D
        ok = (col % NH_D == head_of(row)) & (tok <= row % QSUB) & (tok < n_new)
        _online_softmax_step(m_ref, l_ref, acc_ref, jnp.where(ok, sn, -jnp.inf), vn)
        lam = _lam(lq1_ref, lk1_ref, lq2_ref, lk2_ref)
        o = acc_ref[0:half, :] / l_ref[0:half, :] - lam * (acc_ref[half:R, :] / l_ref[half:R, :])
        o_ref[0] = (_rms(o, subln_ref[...]) * (1.0 - LAMBDA_INIT)).astype(BF16)


def _attn_sample(page_table, q, cache_k, cache_v, knew, vnew, lams, subln, *, n_new):
    B, n_pages = page_table.shape
    rows = cache_k.shape[1]
    R = QMAT_ROWS
    n_pg = math.gcd(n_pages, PAGES_PER_STEP)
    lspec = pl.BlockSpec((1, DH_D), lambda b, p, pt: (0, 0))
    pgs = [pl.BlockSpec((1, rows, LANES), lambda b, p, pt, i=i: (pt[b, p * n_pg + i], 0, 0)) for i in range(n_pg)]
    new = pl.BlockSpec((1, n_new * NH_D, LANES), lambda b, p, pt: (b, 0, 0))
    return pl.pallas_call(
        functools.partial(_attn_sample_body, n_new=n_new, n_pg=n_pg),
        grid_spec=pltpu.PrefetchScalarGridSpec(
            num_scalar_prefetch=1,
            grid=(B, n_pages // n_pg),
            in_specs=[pl.BlockSpec((1, QROWS, D_MODEL), lambda b, p, pt: (b, 0, 0))] + pgs + pgs
                     + [new, new, lspec, lspec, lspec, lspec, pl.BlockSpec((1, DV_D), lambda b, p, pt: (0, 0))],
            out_specs=pl.BlockSpec((1, R // 2, DV_D), lambda b, p, pt: (b, 0, 0)),
            scratch_shapes=[pltpu.VMEM((R, LANES), BF16), pltpu.VMEM((R, rows), F32), pltpu.VMEM((R, 1), F32),
                            pltpu.VMEM((R, 1), F32), pltpu.VMEM((R, DV_D), F32)],
        ),
        out_shape=jax.ShapeDtypeStruct((B, R // 2, DV_D), BF16),
        compiler_params=_cparams(("parallel", "arbitrary")),
        name="attn_sample",
    )(page_table, q, *([cache_k] * n_pg), *([cache_v] * n_pg), knew, vnew, *lams, subln)


def _merge_body(hm_ref, o_ref, ga_ref, gb_ref, x_ref, wa_ref, wb_ref, wo_ref, y_ref):
    ya = _dot(hm_ref[...], wa_ref[...])
    yb = _dot(o_ref[...], wb_ref[...])
    merged = _sigmoid(ga_ref[...]) * ya + _sigmoid(gb_ref[...]) * yb
    y_ref[...] = x_ref[...] + _dot(merged.astype(BF16), wo_ref[...])


def _merge(hm, o, ga, gb, x, wa, wb, wo, *, tm=512):
    n = x.shape[0]
    tm = _row_tile(n, tm)
    row = pl.BlockSpec((tm, D_MODEL), lambda i: (i, 0))
    wspec = _full((D_MODEL, D_MODEL))
    return pl.pallas_call(
        _merge_body,
        grid=(n // tm,),
        in_specs=[row, row, row, row, row, wspec, wspec, wspec],
        out_specs=row,
        out_shape=jax.ShapeDtypeStruct((n, D_MODEL), F32),
        compiler_params=_cparams(("parallel",)),
        name="merge",
    )(hm, o, ga, gb, x, wa, wb, wo)


def _rope_tables(pos):
    half = ROT_DIM // 2
    inv_freq = ROPE_THETA ** (-jnp.arange(half, dtype=F32) / half)
    d = jnp.arange(LANES, dtype=jnp.int32) % DH_D
    freq = jnp.where(d < ROT_DIM, inv_freq[d % half], 0.0)
    ang = pos.astype(F32)[:, None] * freq[None, :]
    c, s = jnp.cos(ang), jnp.sin(ang)
    s_up = jnp.where(d < half, -s, 0.0)
    s_dn = jnp.where(d >= half, s, 0.0)
    return c, s_up, s_dn


def _pad_rows(a, rows, axis, front=False):
    pad = [(0, 0)] * a.ndim
    pad[axis] = (rows - a.shape[axis], 0) if front else (0, rows - a.shape[axis])
    return jnp.pad(a, pad)


def kernel(x_prompt, x_sample, cache_k, cache_v, page_table, state_C, state_n, state_m, state_conv, meta_tokens, ffn1_norm, ffn1_w_gate, ffn1_w_up, ffn1_w_down, mix_norm, w_in, conv_w, conv_b, w_mq, w_mk, w_mv, b_igate, b_fgate, mlstm_norm, lambda_q1, lambda_k1, lambda_q2, lambda_k2, subln, w_proj_a, w_proj_b, w_out, ffn2_norm, ffn2_w_gate, ffn2_w_up, ffn2_w_down, final_norm):
    bp, seq, _ = x_prompt.shape
    bs, ts, _ = x_sample.shape
    n_pages = page_table.shape[1]
    page = cache_k.shape[2]
    past_len = n_pages * page
    assert w_in.shape[0] == 1 and ts <= QSUB

    win = w_in[0]
    offs = [0]
    for width in (D_MODEL, D_MODEL, NH_M, NH_M, D_MODEL, D_MODEL, D_MODEL, D_MODEL, D_MODEL):
        offs.append(offs[-1] + width)
    col = lambda i: win[:, offs[i]:offs[i + 1]]
    wgate = jnp.pad(jnp.concatenate([col(2), col(3)], axis=1), ((0, 0), (0, LANES - 2 * NH_M)))
    wproj = {"xm": col(0).astype(BF16), "om": col(1).astype(BF16), "q": col(4).astype(BF16),
             "k": col(5).astype(BF16), "v": col(6).astype(BF16), "ga": col(7).astype(BF16),
             "gb": col(8).astype(BF16), "gate": wgate.astype(BF16)}
    gbias = jnp.pad(jnp.concatenate([b_igate[0], b_fgate[0]])[None, :], ((0, 0), (0, LANES - 2 * NH_M)))
    wm = {"conv_w": conv_w[0], "conv_b": conv_b[0][None, :],
          "mq": w_mq[0].astype(BF16), "mk": (w_mk[0] * DH_M ** -0.5).astype(BF16), "mv": w_mv[0].astype(BF16),
          "mvt": jnp.swapaxes(w_mv[0], 1, 2).astype(BF16), "mnorm": mlstm_norm[0].reshape(1, D_MODEL)}
    lams = (lambda_q1, lambda_k1, lambda_q2, lambda_k2)
    row = lambda a: a.reshape(1, -1)
    f1 = (row(ffn1_norm[0]), ffn1_w_gate[0].astype(BF16), ffn1_w_up[0].astype(BF16), ffn1_w_down[0].astype(BF16))
    f2 = (row(ffn2_norm[0]), ffn2_w_gate[0].astype(BF16), ffn2_w_up[0].astype(BF16), ffn2_w_down[0].astype(BF16))
    fin = row(final_norm)
    wa, wb, wo = w_proj_a[0].astype(BF16), w_proj_b[0].astype(BF16), w_out[0].astype(BF16)

    n_main = bp * seq
    n_samp = bs * ts
    x_main = x_prompt.reshape(n_main, D_MODEL)
    x_small = jnp.concatenate([x_sample.reshape(n_samp, D_MODEL), meta_tokens], axis=0)

    tm_in = min(seq, 256)
    tabs_main = _rope_tables(N_META + jnp.arange(seq, dtype=jnp.int32))
    pos_small = jnp.concatenate([jnp.tile(past_len + jnp.arange(ts, dtype=jnp.int32), bs),
                                 jnp.arange(N_META, dtype=jnp.int32)])
    tabs_small = _rope_tables(pos_small)

    x1_main = _ffn(x_main, *f1, fin, final_norm=False)
    x1_small = _ffn(x_small, *f1, fin, final_norm=False)
    pm = _inproj(x1_main, row(mix_norm[0]), wproj, gbias, tabs_main, n_rope_blocks=seq // tm_in, tm=tm_in)
    ps = _inproj(x1_small, row(mix_norm[0]), wproj, gbias, tabs_small, n_rope_blocks=1, tm=x_small.shape[0])
    xm_m, om_m, q_m, k_m, v_m, ga_m, gb_m, gcol_m, grow_m = pm
    xm_s, om_s, q_s, k_s, v_s, ga_s, gb_s, gcol_s, grow_s = ps

    def split(a):
        return a[:n_samp], a[n_samp:]

    xm_smp, xm_meta = split(xm_s)
    om_smp, om_meta = split(om_s)
    gcol_smp, gcol_meta = split(gcol_s)
    grow_smp, grow_meta = grow_s[:, :n_samp], grow_s[:, n_samp:]

    zc = jnp.zeros((1, NH_M, DH_M, DH_M), F32)
    zn = jnp.zeros((1, SUBLANES, DH_M), F32)
    zm = jnp.zeros((1, SUBLANES, LANES), F32)
    zbuf = jnp.zeros((1, SUBLANES, D_MODEL), F32)
    hm_meta, c_meta, n_meta, m_meta = _mlstm(
        xm_meta[None], om_meta[None], gcol_meta[None], grow_meta.reshape(1, 1, SUBLANES, N_META),
        zbuf, zc, zn, zm, wm, chunk=N_META, nb=1)

    chunk = _row_tile(seq, MLSTM_CHUNK)
    grow_main = grow_m.reshape(SUBLANES, bp, seq // chunk, chunk).transpose(1, 2, 0, 3)
    buf_main = _pad_rows(xm_meta[N_META - (CONV_W - 1):], SUBLANES, 0, front=True)[None]
    hm_main, c_main, n_main_st, m_main = _mlstm(
        xm_m.reshape(bp, seq, D_MODEL), om_m.reshape(bp, seq, D_MODEL), gcol_m.reshape(bp, seq, LANES),
        grow_main, buf_main, c_meta, n_meta, m_meta, wm, chunk=chunk, nb=_group(bp))

    pad_t = lambda a: _pad_rows(a.reshape(bs, ts, a.shape[-1]), QROWS, 1)
    gcol_pad = jnp.concatenate(
        [gcol_smp.reshape(bs, ts, LANES),
         jnp.broadcast_to(jnp.where(jnp.arange(LANES) < NH_M, NEG_BIG, 0.0).astype(F32), (bs, QROWS - ts, LANES))],
        axis=1)
    grow_pad = jnp.concatenate(
        [grow_smp.reshape(SUBLANES, bs, ts),
         jnp.broadcast_to(jnp.where(jnp.arange(SUBLANES) < NH_M, NEG_BIG, 0.0).astype(F32)[:, None, None],
                          (SUBLANES, bs, QROWS - ts))], axis=2).transpose(1, 0, 2)[:, None]
    hm_smp, c_smp, n_smp, m_smp = _mlstm(
        pad_t(xm_smp), pad_t(om_smp), gcol_pad, grow_pad,
        _pad_rows(state_conv[0], SUBLANES, 1, front=True), state_C[0],
        _pad_rows(state_n[0], SUBLANES, 1), jnp.broadcast_to(_pad_rows(state_m[0], SUBLANES, 1)[:, :, None],
                                                             (bs, SUBLANES, LANES)),
        wm, chunk=QROWS, nb=_group(bs))

    q_smp, q_meta = split(q_s)
    k_smp, k_meta = split(k_s)
    v_smp, v_meta = split(v_s)
    sub = row(subln[0])
    sub_col = subln[0].reshape(DV_D, 1)
    o_meta = _attn_prompt(q_meta[None], k_meta[None], v_meta[None], k_meta, v_meta, lams, sub_col,
                          tq=N_META, has_prefix=False)
    tq = min(seq, ATTN_TQ)
    o_main, k_full, v_full = _attn_prompt(q_m.reshape(bp, seq, D_MODEL), k_m.reshape(bp, seq, D_MODEL),
                                          v_m.reshape(bp, seq, D_MODEL), k_meta, v_meta, lams, sub_col,
                                          tq=tq, has_prefix=True)
    del o_meta
    o_smp = _attn_sample(page_table, pad_t(q_smp), cache_k.reshape(-1, page * NH_D, DV_D),
                         cache_v.reshape(-1, page * NH_D, DV_D), k_smp.reshape(bs, ts * NH_D, DV_D),
                         v_smp.reshape(bs, ts * NH_D, DV_D), lams, sub, n_new=ts)
    o_smp = o_smp.reshape(bs, NH_D, QSUB, DV_D)[:, :, :ts].transpose(0, 2, 1, 3).reshape(n_samp, D_MODEL)

    x2_main = _merge(hm_main.reshape(n_main, D_MODEL), o_main.reshape(n_main, D_MODEL), ga_m, gb_m, x1_main,
                     wa, wb, wo)
    y_main = _ffn(x2_main, *f2, fin, final_norm=True)
    x2_smp = _merge(hm_smp[:, :ts].reshape(n_samp, D_MODEL), o_smp,
                    ga_s[:n_samp], gb_s[:n_samp], x1_small[:n_samp], wa, wb, wo)
    y_smp = _ffn(x2_smp, *f2, fin, final_norm=True)
    del hm_meta

    xm_main3 = xm_m.reshape(bp, seq, D_MODEL)
    return (
        y_main.reshape(bp, seq, D_MODEL),
        y_smp.reshape(bs, ts, D_MODEL),
        k_full.reshape(1, bp, seq + N_META, NH_D, DV_D),
        v_full.reshape(1, bp, seq + N_META, NH_D, DV_D),
        c_main[None],
        n_main_st[None, :, :NH_M, :],
        m_main[None, :, :NH_M, 0],
        xm_main3[None, :, seq - (CONV_W - 1):, :],
        k_smp.reshape(1, bs, ts, NH_D, DV_D),
        v_smp.reshape(1, bs, ts, NH_D, DV_D),
        c_smp[None],
        n_smp[None, :, :NH_M, :],
        m_smp[None, :, :NH_M, 0],
        xm_smp.reshape(bs, ts, D_MODEL)[None, :, ts - (CONV_W - 1):, :],
    )
```

```python
import functools
import math

import jax
import jax.numpy as jnp
from jax import lax
from jax.experimental import pallas as pl
from jax.experimental.pallas import tpu as pltpu

F32 = jnp.float32
BF16 = jnp.bfloat16

D_MODEL = 1024
N_META = 16
EPS = 1e-6
D_FF = 2816
NH_M = 4
DH_M = 256
CONV_W = 4
NH_D = 8
DH_D = 64
DV_D = 128
ROT_DIM = 16
ROPE_THETA = 500000.0
LAMBDA_INIT = 0.8 - 0.6 * math.exp(-0.3 * 0)

LANES = 128
SUBLANES = 8
NEG_BIG = -1e30
VMEM_LIMIT = 56 * 1024 * 1024


def _cparams(sem):
    return pltpu.CompilerParams(dimension_semantics=sem, vmem_limit_bytes=VMEM_LIMIT)


def _rms(x, g):
    return x * lax.rsqrt(jnp.mean(x * x, axis=-1, keepdims=True) + EPS) * g


def _sigmoid(x):
    return 1.0 / (1.0 + jnp.exp(-x))


def _dot(a, b):
    return jnp.dot(a, b, preferred_element_type=F32)


def _dot_nt(a, b):
    return lax.dot_general(a, b, (((1,), (1,)), ((), ())), preferred_element_type=F32)


def _row_tile(n, want):
    if n <= 2 * want:
        return n
    t = want
    while n % t:
        t //= 2
    assert t == n or t % SUBLANES == 0
    return t


MLSTM_CHUNK = 256
MLSTM_GROUP = 4


def _group(n):
    return max(g for g in range(1, MLSTM_GROUP + 1) if n % g == 0)


def _full(shape):
    zeros = (0,) * len(shape)
    return pl.BlockSpec(shape, lambda *_: zeros)


def _resident(shape):
    zeros = (0,) * len(shape)
    return pl.BlockSpec(shape, lambda *_: zeros, pipeline_mode=pl.Buffered(1))


def _ffn_body(x_ref, g_ref, wg_ref, wu_ref, wd_ref, fg_ref, o_ref, *, final_norm):
    x = x_ref[...]
    xb = _rms(x, g_ref[...]).astype(BF16)
    g = _dot(xb, wg_ref[...])
    u = _dot(xb, wu_ref[...])
    a = (g * _sigmoid(g) * u).astype(BF16)
    y = x + 0.5 * _dot(a, wd_ref[...])
    if final_norm:
        y = _rms(y, fg_ref[...])
    o_ref[...] = y


def _ffn(x, gain, wg, wu, wd, final_gain, *, final_norm, tm=1024):
    n = x.shape[0]
    tm = _row_tile(n, tm)
    row = pl.BlockSpec((tm, D_MODEL), lambda i: (i, 0))
    return pl.pallas_call(
        functools.partial(_ffn_body, final_norm=final_norm),
        grid=(n // tm,),
        in_specs=[row, _full((1, D_MODEL)), _resident(wg.shape), _resident(wu.shape), _resident(wd.shape),
                  _full((1, D_MODEL))],
        out_specs=row,
        out_shape=jax.ShapeDtypeStruct((n, D_MODEL), F32),
        compiler_params=_cparams(("parallel",)),
        name="ffn_final" if final_norm else "ffn",
    )(x, gain, wg, wu, wd, final_gain)


def _rope_slab(xs, c, s_up, s_dn):
    return xs * c + pltpu.roll(xs, LANES - ROT_DIM // 2, axis=1) * s_up + pltpu.roll(xs, ROT_DIM // 2, axis=1) * s_dn


def _inproj_body(x_ref, g_ref, wxm_ref, wom_ref, wq_ref, wk_ref, wv_ref, wga_ref, wgb_ref, wgate_ref,
                 gbias_ref, cq_ref, cu_ref, cd_ref,
                 xm_ref, om_ref, q_ref, k_ref, v_ref, ga_ref, gb_ref, gcol_ref, grow_ref):
    hb = _rms(x_ref[...], g_ref[...]).astype(BF16)
    xm_ref[...] = _dot(hb, wxm_ref[...])
    om_ref[...] = _dot(hb, wom_ref[...])
    v_ref[...] = _dot(hb, wv_ref[...])
    ga_ref[...] = _dot(hb, wga_ref[...])
    gb_ref[...] = _dot(hb, wgb_ref[...])
    c, s_up, s_dn = cq_ref[...], cu_ref[...], cd_ref[...]
    q = _dot(hb, wq_ref[...])
    k = _dot(hb, wk_ref[...])
    qscale = DH_D ** -0.5 * math.log2(math.e)
    for h in range(NH_D):
        sl = slice(h * LANES, (h + 1) * LANES)
        q_ref[:, sl] = (_rope_slab(q[:, sl], c, s_up, s_dn) * qscale).astype(BF16)
        k_ref[:, sl] = _rope_slab(k[:, sl], c, s_up, s_dn)
    pre = _dot(hb, wgate_ref[...]) + gbias_ref[...]
    lane = lax.broadcasted_iota(jnp.int32, pre.shape, 1)
    logsig = jnp.minimum(pre, 0.0) - jnp.log(1.0 + jnp.exp(-jnp.abs(pre)))
    glog = jnp.where(lane < NH_M, pre, logsig)
    gcol_ref[...] = glog
    grow_ref[...] = glog.T[:SUBLANES, :]


def _inproj(x, gain, w, gbias, rope_tabs, *, n_rope_blocks, tm):
    n = x.shape[0]
    assert n % tm == 0
    row = pl.BlockSpec((tm, D_MODEL), lambda i: (i, 0))
    tab = pl.BlockSpec((tm, LANES), lambda i: (i % n_rope_blocks, 0))
    wspec = _resident((D_MODEL, D_MODEL))
    out_shapes = (
        jax.ShapeDtypeStruct((n, D_MODEL), F32),
        jax.ShapeDtypeStruct((n, D_MODEL), F32),
        jax.ShapeDtypeStruct((n, D_MODEL), BF16),
        jax.ShapeDtypeStruct((n, D_MODEL), F32),
        jax.ShapeDtypeStruct((n, D_MODEL), F32),
        jax.ShapeDtypeStruct((n, D_MODEL), F32),
        jax.ShapeDtypeStruct((n, D_MODEL), F32),
        jax.ShapeDtypeStruct((n, LANES), F32),
        jax.ShapeDtypeStruct((SUBLANES, n), F32),
    )
    return pl.pallas_call(
        _inproj_body,
        grid=(n // tm,),
        in_specs=[row, _full((1, D_MODEL))] + [wspec] * 7 + [_full((D_MODEL, LANES)), _full((1, LANES)),
                                                              tab, tab, tab],
        out_specs=(row, row, row, row, row, row, row,
                   pl.BlockSpec((tm, LANES), lambda i: (i, 0)),
                   pl.BlockSpec((SUBLANES, tm), lambda i: (0, i))),
        out_shape=out_shapes,
        compiler_params=_cparams(("parallel",)),
        name="inproj",
    )(x, gain, w["xm"], w["om"], w["q"], w["k"], w["v"], w["ga"], w["gb"], w["gate"], gbias, *rope_tabs)


def _mlstm_body(xm_ref, xprev_ref, buf0_ref, om_ref, gcol_ref, grow_ref, cw_ref, cb_ref,
                wq_ref, wk_ref, wv_ref, wvt_ref, norm_ref, c0_ref, n0_ref, m0_ref,
                hm_ref, c_ref, n_ref, m_ref, ext_ref, *, chunk, shared_init):
    ci = pl.program_id(1)
    L = chunk
    nb = xm_ref.shape[0]

    @pl.when(ci == 0)
    def _():
        for bi in range(nb):
            src = 0 if shared_init else bi
            c_ref[bi] = c0_ref[src]
            n_ref[bi] = n0_ref[src]
            m_ref[bi] = m0_ref[src]
            ext_ref[bi, 0:SUBLANES, :] = buf0_ref[src]

    @pl.when(ci > 0)
    def _():
        ext_ref[:, 0:SUBLANES, :] = xprev_ref[...]

    x = xm_ref[...]
    ext_ref[:, SUBLANES:SUBLANES + L, :] = x
    cv = cb_ref[...]
    for j in range(CONV_W):
        cv = cv + ext_ref[:, pl.ds(SUBLANES - (CONV_W - 1) + j, L), :] * cw_ref[j:j + 1, :]
    cact = cv * _sigmoid(cv)

    ti = lax.broadcasted_iota(jnp.int32, (L, L), 0)
    si = lax.broadcasted_iota(jnp.int32, (L, L), 1)
    tril = si <= ti

    for h in range(NH_M):
        hs = slice(h * DH_M, (h + 1) * DH_M)
        chb = cact[:, :, hs].reshape(nb * L, DH_M).astype(BF16)
        xhb = x[:, :, hs].reshape(nb * L, DH_M).astype(BF16)
        q_all = _dot(chb, wq_ref[h])
        k_all = _dot(chb, wk_ref[h])
        v_all = _dot(xhb, wv_ref[h])
        for bi in range(nb):
            rows = slice(bi * L, (bi + 1) * L)
            q, k, v = q_all[rows], k_all[rows], v_all[rows]
            vt = _dot_nt(wvt_ref[h], xhb[rows])
            qb, kb, vb = q.astype(BF16), k.astype(BF16), v.astype(BF16)

            gcol = gcol_ref[bi]
            grow = grow_ref[bi, 0]
            logi_c, logf_c = gcol[:, h:h + 1], gcol[:, NH_M + h:NH_M + h + 1]
            logi_r, logf_r = grow[h:h + 1, :], grow[NH_M + h:NH_M + h + 1, :]
            b_c = jnp.sum(jnp.where(tril, logf_r, 0.0), axis=1, keepdims=True)
            b_r = jnp.sum(jnp.where(ti <= si, logf_c, 0.0), axis=0, keepdims=True)
            m_prev = m_ref[bi, h:h + 1, 0:1]
            n_prev = n_ref[bi, h:h + 1, :]
            c_prev = c_ref[bi, h]

            d = jnp.where(tril, b_c - b_r + logi_r, -jnp.inf)
            inter = b_c + m_prev
            m_t = jnp.maximum(jnp.max(d, axis=1, keepdims=True), inter)
            w_intra = jnp.exp(d - m_t)
            w_inter = jnp.exp(inter - m_t)
            s = _dot_nt(qb, kb) * w_intra
            num = _dot(s.astype(BF16), vb) + w_inter * _dot_nt(qb, c_prev.astype(BF16))
            den = jnp.sum(s, axis=1, keepdims=True) + w_inter * jnp.sum(q * n_prev, axis=1, keepdims=True)
            hh = num / jnp.maximum(jnp.abs(den), jnp.exp(-m_t))

            m_new = m_t[L - 1:L, :]
            b_last = b_c[L - 1:L, :]
            w_state_r = jnp.exp(b_last - b_r + logi_r - m_new)
            w_state_c = jnp.exp(b_last - b_c + logi_c - m_new)
            decay = jnp.exp(b_last + m_prev - m_new)
            c_ref[bi, h] = decay * c_prev + _dot((vt * w_state_r).astype(BF16), kb)
            n_ref[bi, h:h + 1, :] = decay * n_prev + jnp.sum(k * w_state_c, axis=0, keepdims=True)
            m_ref[bi, h:h + 1, :] = jnp.broadcast_to(m_new, (1, LANES))

            hn = hh * lax.rsqrt(jnp.mean(hh * hh, axis=1, keepdims=True) + EPS) * norm_ref[:, hs]
            hm_ref[bi, :, hs] = (hn * _sigmoid(om_ref[bi, :, hs])).astype(BF16)


def _mlstm(xm, om, gcol, grow, buf0, c0, n0, m0, w, *, chunk, nb):
    B, T, _ = xm.shape
    L = chunk
    nc = T // L
    shared = c0.shape[0] == 1 and B > 1
    nb0 = 1 if shared else nb
    st = (lambda b, c: (0, 0, 0)) if shared else (lambda b, c: (b, 0, 0))
    st4 = (lambda b, c: (0, 0, 0, 0)) if shared else (lambda b, c: (b, 0, 0, 0))
    seq = pl.BlockSpec((nb, L, D_MODEL), lambda b, c: (b, c, 0))
    wsq = _full((NH_M, DH_M, DH_M))
    return pl.pallas_call(
        functools.partial(_mlstm_body, chunk=L, shared_init=shared),
        grid=(B // nb, nc),
        in_specs=[
            seq,
            pl.BlockSpec((nb, SUBLANES, D_MODEL), lambda b, c: (b, jnp.maximum(c * (L // SUBLANES) - 1, 0), 0)),
            pl.BlockSpec((nb0, SUBLANES, D_MODEL), st),
            seq,
            pl.BlockSpec((nb, L, LANES), lambda b, c: (b, c, 0)),
            pl.BlockSpec((nb, 1, SUBLANES, L), lambda b, c: (b, c, 0, 0)),
            _full((CONV_W, D_MODEL)), _full((1, D_MODEL)),
            wsq, wsq, wsq, wsq, _full((1, D_MODEL)),
            pl.BlockSpec((nb0, NH_M, DH_M, DH_M), st4),
            pl.BlockSpec((nb0, SUBLANES, DH_M), st),
            pl.BlockSpec((nb0, SUBLANES, LANES), st),
        ],
        out_specs=(
            seq,
            pl.BlockSpec((nb, NH_M, DH_M, DH_M), lambda b, c: (b, 0, 0, 0)),
            pl.BlockSpec((nb, SUBLANES, DH_M), lambda b, c: (b, 0, 0)),
            pl.BlockSpec((nb, SUBLANES, LANES), lambda b, c: (b, 0, 0)),
        ),
        out_shape=(
            jax.ShapeDtypeStruct((B, T, D_MODEL), BF16),
            jax.ShapeDtypeStruct((B, NH_M, DH_M, DH_M), F32),
            jax.ShapeDtypeStruct((B, SUBLANES, DH_M), F32),
            jax.ShapeDtypeStruct((B, SUBLANES, LANES), F32),
        ),
        scratch_shapes=[pltpu.VMEM((nb, SUBLANES + L, D_MODEL), F32)],
        compiler_params=_cparams(("parallel", "arbitrary")),
        name="mlstm",
    )(xm, xm, buf0, om, gcol, grow, w["conv_w"], w["conv_b"], w["mq"], w["mk"], w["mv"], w["mvt"],
      w["mnorm"], c0, n0, m0)


def _lam(lq1_ref, lk1_ref, lq2_ref, lk2_ref):
    return (jnp.exp(jnp.sum(lq1_ref[...] * lk1_ref[...], axis=1, keepdims=True))
            - jnp.exp(jnp.sum(lq2_ref[...] * lk2_ref[...], axis=1, keepdims=True)) + LAMBDA_INIT)


def _online_softmax_step(m_ref, l_ref, acc_ref, s, vb):
    m_old = m_ref[...]
    m_new = jnp.maximum(m_old, jnp.max(s, axis=1, keepdims=True))
    pr = jnp.exp2(s - m_new)
    r = jnp.exp2(m_old - m_new)
    l_ref[...] = r * l_ref[...] + jnp.sum(pr, axis=1, keepdims=True)
    acc_ref[...] = r * acc_ref[...] + _dot(pr.astype(BF16), vb)
    m_ref[...] = m_new


ATTN_TQ = 1024
ONES_ROWS = 16


def _attn_prompt_body(q_ref, k_ref, v_ref, kp_ref, vp_ref, vpt_ref, lq1_ref, lk1_ref, lq2_ref, lk2_ref, subln_ref,
                      o_ref, kf_ref, vf_ref, kb_ref, vt_ref, m_ref, acc_ref, sa_ref, sb_ref, ma_ref, mb_ref, *, tq, tk,
                      has_prefix):
    qi = pl.program_id(2)
    nblk = vt_ref.shape[0]

    def with_ones(vt):
        keys = vt.shape[1]
        row = lax.broadcasted_iota(jnp.int32, (ONES_ROWS, keys), 0)
        return jnp.concatenate([vt, jnp.where(row == 0, 1.0, 0.0)], axis=0).astype(BF16)

    @pl.when(qi == 0)
    def _():
        npre = kp_ref.shape[0]
        kf_ref[0, 0:npre, :] = kp_ref[...]
        vf_ref[0, 0:npre, :] = vp_ref[...]
        kf_ref[0, npre:, :] = k_ref[0]
        vf_ref[0, npre:, :] = v_ref[0]
        kb_ref[...] = k_ref[0].astype(BF16)
        for c in range(nblk):
            vt_ref[c] = with_ones(v_ref[0, c * tk:(c + 1) * tk, :].T)

    q = q_ref[0].astype(F32)
    lane = lax.broadcasted_iota(jnp.int32, q.shape, 1)
    qq = jnp.concatenate([jnp.where(lane < DH_D, q, 0.0), jnp.where(lane < DH_D, 0.0, q)], axis=0)
    qqt = qq.T.astype(BF16)
    m_ref[...] = jnp.full(m_ref.shape, -jnp.inf, F32)
    acc_ref[...] = jnp.zeros(acc_ref.shape, F32)

    def apply(st, mx, vt):
        m_old = m_ref[...]
        m_new = jnp.maximum(m_old, mx)
        acc_ref[...] = jnp.exp2(m_old - m_new) * acc_ref[...] + _dot(vt, jnp.exp2(st - m_new).astype(BF16))
        m_ref[...] = m_new

    def step(buf, j):
        s_ref, mx_ref = buf
        apply(s_ref[...], mx_ref[...], vt_ref[j])

    rel = (lax.broadcasted_iota(jnp.int32, (tk, 2 * tq), 0)
           - lax.broadcasted_iota(jnp.int32, (tk, 2 * tq), 1) % tq)

    def put(buf, j, masked):
        s_ref, mx_ref = buf
        st = _dot(kb_ref[pl.ds(pl.multiple_of(j * tk, tk), tk), :], qqt)
        if masked:
            st = jnp.where(rel <= qi * tq - j * tk, st, -jnp.inf)
        s_ref[...] = st
        mx_ref[...] = jnp.max(st, axis=0, keepdims=True)

    if has_prefix:
        st = _dot(kp_ref[...].astype(BF16), qqt)
        apply(st, jnp.max(st, axis=0, keepdims=True), with_ones(vpt_ref[...]))

    assert tq == 2 * tk
    buf_a, buf_b = (sa_ref, ma_ref), (sb_ref, mb_ref)
    put(buf_a, 0, True)

    def pair(p, carry):
        j = 2 * p
        put(buf_b, j + 1, False)
        step(buf_a, j)
        put(buf_a, j + 2, False)
        step(buf_b, j + 1)
        return carry

    lax.fori_loop(0, qi - 1, pair, 0)

    @pl.when(qi >= 1)
    def _():
        j = 2 * (qi - 1)
        put(buf_b, j + 1, False)
        step(buf_a, j)
        put(buf_a, j + 2, True)
        step(buf_b, j + 1)

    put(buf_b, 2 * qi + 1, True)
    step(buf_a, 2 * qi)
    step(buf_b, 2 * qi + 1)

    lam = _lam(lq1_ref, lk1_ref, lq2_ref, lk2_ref)
    acc = acc_ref[0:DV_D, :] / acc_ref[DV_D:DV_D + 1, :]
    ot = acc[:, 0:tq] - lam * acc[:, tq:2 * tq]
    ot = ot * lax.rsqrt(jnp.mean(ot * ot, axis=0, keepdims=True) + EPS) * subln_ref[...]
    o_ref[0] = (ot * (1.0 - LAMBDA_INIT)).T.astype(BF16)


def _attn_prompt(q, k, v, kpre, vpre, lams, subln_col, *, tq, has_prefix):
    vpre_t = vpre.T
    B, T, _ = q.shape
    P = kpre.shape[0]
    tk = tq // 2
    assert T % tq == 0 and tq % (2 * tk) == 0
    lspec = _full((1, DH_D))
    return pl.pallas_call(
        functools.partial(_attn_prompt_body, tq=tq, tk=tk, has_prefix=has_prefix),
        grid=(B, NH_D, T // tq),
        in_specs=[
            pl.BlockSpec((1, tq, LANES), lambda b, h, i: (b, i, h)),
            pl.BlockSpec((1, T, LANES), lambda b, h, i: (b, 0, h)),
            pl.BlockSpec((1, T, LANES), lambda b, h, i: (b, 0, h)),
            pl.BlockSpec((P, LANES), lambda b, h, i: (0, h)),
            pl.BlockSpec((P, LANES), lambda b, h, i: (0, h)),
            pl.BlockSpec((DV_D, P), lambda b, h, i: (h, 0)),
            lspec, lspec, lspec, lspec, _full((DV_D, 1)),
        ],
        out_specs=(pl.BlockSpec((1, tq, LANES), lambda b, h, i: (b, i, h)),
                   pl.BlockSpec((1, P + T, LANES), lambda b, h, i: (b, 0, h)),
                   pl.BlockSpec((1, P + T, LANES), lambda b, h, i: (b, 0, h))),
        out_shape=(jax.ShapeDtypeStruct((B, T, D_MODEL), BF16),
                   jax.ShapeDtypeStruct((B, P + T, D_MODEL), F32),
                   jax.ShapeDtypeStruct((B, P + T, D_MODEL), F32)),
        scratch_shapes=[pltpu.VMEM((T, LANES), BF16), pltpu.VMEM((T // tk, DV_D + ONES_ROWS, tk), BF16),
                        pltpu.VMEM((1, 2 * tq), F32), pltpu.VMEM((DV_D + ONES_ROWS, 2 * tq), F32),
                        pltpu.VMEM((tk, 2 * tq), F32), pltpu.VMEM((tk, 2 * tq), F32),
                        pltpu.VMEM((1, 2 * tq), F32), pltpu.VMEM((1, 2 * tq), F32)],
        compiler_params=_cparams(("parallel", "parallel", "arbitrary")),
        name="attn_prompt",
    )(q, k, v, kpre, vpre, vpre_t, *lams, subln_col)


QROWS = 8
QSUB = 4
QMAT_ROWS = 2 * NH_D * QSUB
PAGES_PER_STEP = 16


def _attn_sample_body(pt_ref, q_ref, *refs, n_new, n_pg):
    kc_refs, vc_refs = refs[:n_pg], refs[n_pg:2 * n_pg]
    (kn_ref, vn_ref, lq1_ref, lk1_ref, lq2_ref, lk2_ref, subln_ref,
     o_ref, qm_ref, bias_ref, m_ref, l_ref, acc_ref) = refs[2 * n_pg:]
    p = pl.program_id(1)
    R = QMAT_ROWS
    rows = kc_refs[0].shape[1]
    half = R // 2

    def head_of(r):
        return (r % half) // QSUB

    @pl.when(p == 0)
    def _():
        q = q_ref[0].astype(F32)
        sub = lax.broadcasted_iota(jnp.int32, (QROWS, LANES), 0)
        pieces = []
        for j in range(NH_D // 2):
            even = q[:, (2 * j) * LANES:(2 * j + 1) * LANES]
            odd = q[:, (2 * j + 1) * LANES:(2 * j + 2) * LANES]
            pieces.append(jnp.where(sub < QSUB, even, pltpu.roll(odd, QSUB, axis=0)))
        qh = jnp.concatenate(pieces, axis=0)
        lane = lax.broadcasted_iota(jnp.int32, qh.shape, 1)
        qm_ref[...] = jnp.concatenate([jnp.where(lane < DH_D, qh, 0.0), jnp.where(lane < DH_D, 0.0, qh)],
                                      axis=0).astype(BF16)
        row = lax.broadcasted_iota(jnp.int32, (R, rows), 0)
        col = lax.broadcasted_iota(jnp.int32, (R, rows), 1)
        bias_ref[...] = jnp.where(col % NH_D == head_of(row), 0.0, -jnp.inf)
        m_ref[...] = jnp.full(m_ref.shape, -jnp.inf, F32)
        l_ref[...] = jnp.zeros(l_ref.shape, F32)
        acc_ref[...] = jnp.zeros(acc_ref.shape, F32)

    qm = qm_ref[...]
    bias = bias_ref[...]
    s = [_dot_nt(qm, kc[0].astype(BF16)) + bias for kc in kc_refs]
    m_old = m_ref[...]
    m_new = m_old
    for si in s:
        m_new = jnp.maximum(m_new, jnp.max(si, axis=1, keepdims=True))
    r = jnp.exp2(m_old - m_new)
    l_new = r * l_ref[...]
    acc = r * acc_ref[...]
    for si, vc in zip(s, vc_refs):
        pr = jnp.exp2(si - m_new)
        l_new = l_new + jnp.sum(pr, axis=1, keepdims=True)
        acc = acc + _dot(pr.astype(BF16), vc[0].astype(BF16))
    m_ref[...] = m_new
    l_ref[...] = l_new
    acc_ref[...] = acc

    @pl.when(p == pl.num_programs(1) - 1)
    def _():
        zpad = jnp.zeros((LANES - kn_ref.shape[1], LANES), F32)
        kn = jnp.concatenate([kn_ref[0], zpad], axis=0).astype(BF16)
        vn = jnp.concatenate([vn_ref[0], zpad], axis=0).astype(BF16)
        sn = _dot_nt(qm, kn)
        row = lax.broadcasted_iota(jnp.int32, sn.shape, 0)
        col = lax.broadcasted_iota(jnp.int32, sn.shape, 1)
        tok = col // NH_D
        ok = (col % NH_D == head_of(row)) & (tok <= row % QSUB) & (tok < n_new)
        _online_softmax_step(m_ref, l_ref, acc_ref, jnp.where(ok, sn, -jnp.inf), vn)
        lam = _lam(lq1_ref, lk1_ref, lq2_ref, lk2_ref)
        o = acc_ref[0:half, :] / l_ref[0:half, :] - lam * (acc_ref[half:R, :] / l_ref[half:R, :])
        o_ref[0] = (_rms(o, subln_ref[...]) * (1.0 - LAMBDA_INIT)).astype(BF16)


def _attn_sample(page_table, q, cache_k, cache_v, knew, vnew, lams, subln, *, n_new):
    B, n_pages = page_table.shape
    rows = cache_k.shape[1]
    R = QMAT_ROWS
    n_pg = math.gcd(n_pages, PAGES_PER_STEP)
    lspec = pl.BlockSpec((1, DH_D), lambda b, p, pt: (0, 0))
    pgs = [pl.BlockSpec((1, rows, LANES), lambda b, p, pt, i=i: (pt[b, p * n_pg + i], 0, 0)) for i in range(n_pg)]
    new = pl.BlockSpec((1, n_new * NH_D, LANES), lambda b, p, pt: (b, 0, 0))
    return pl.pallas_call(
        functools.partial(_attn_sample_body, n_new=n_new, n_pg=n_pg),
        grid_spec=pltpu.PrefetchScalarGridSpec(
            num_scalar_prefetch=1,
            grid=(B, n_pages // n_pg),
            in_specs=[pl.BlockSpec((1, QROWS, D_MODEL), lambda b, p, pt: (b, 0, 0))] + pgs + pgs
                     + [new, new, lspec, lspec, lspec, lspec, pl.BlockSpec((1, DV_D), lambda b, p, pt: (0, 0))],
            out_specs=pl.BlockSpec((1, R // 2, DV_D), lambda b, p, pt: (b, 0, 0)),
            scratch_shapes=[pltpu.VMEM((R, LANES), BF16), pltpu.VMEM((R, rows), F32), pltpu.VMEM((R, 1), F32),
                            pltpu.VMEM((R, 1), F32), pltpu.VMEM((R, DV_D), F32)],
        ),
        out_shape=jax.ShapeDtypeStruct((B, R // 2, DV_D), BF16),
        compiler_params=_cparams(("parallel", "arbitrary")),
        name="attn_sample",
    )(page_table, q, *([cache_k] * n_pg), *([cache_v] * n_pg), knew, vnew, *lams, subln)


def _merge_body(hm_ref, o_ref, ga_ref, gb_ref, x_ref, wa_ref, wb_ref, wo_ref, y_ref):
    ya = _dot(hm_ref[...], wa_ref[...])
    yb = _dot(o_ref[...], wb_ref[...])
    merged = _sigmoid(ga_ref[...]) * ya + _sigmoid(gb_ref[...]) * yb
    y_ref[...] = x_ref[...] + _dot(merged.astype(BF16), wo_ref[...])


def _merge(hm, o, ga, gb, x, wa, wb, wo, *, tm=1024):
    n = x.shape[0]
    tm = _row_tile(n, tm)
    row = pl.BlockSpec((tm, D_MODEL), lambda i: (i, 0))
    wspec = _resident((D_MODEL, D_MODEL))
    return pl.pallas_call(
        _merge_body,
        grid=(n // tm,),
        in_specs=[row, row, row, row, row, wspec, wspec, wspec],
        out_specs=row,
        out_shape=jax.ShapeDtypeStruct((n, D_MODEL), F32),
        compiler_params=_cparams(("parallel",)),
        name="merge",
    )(hm, o, ga, gb, x, wa, wb, wo)


def _rope_tables(pos):
    half = ROT_DIM // 2
    inv_freq = ROPE_THETA ** (-jnp.arange(half, dtype=F32) / half)
    d = jnp.arange(LANES, dtype=jnp.int32) % DH_D
    freq = jnp.where(d < ROT_DIM, inv_freq[d % half], 0.0)
    ang = pos.astype(F32)[:, None] * freq[None, :]
    c, s = jnp.cos(ang), jnp.sin(ang)
    s_up = jnp.where(d < half, -s, 0.0)
    s_dn = jnp.where(d >= half, s, 0.0)
    return c, s_up, s_dn


def _pad_rows(a, rows, axis, front=False):
    pad = [(0, 0)] * a.ndim
    pad[axis] = (rows - a.shape[axis], 0) if front else (0, rows - a.shape[axis])
    return jnp.pad(a, pad)


def kernel(x_prompt, x_sample, cache_k, cache_v, page_table, state_C, state_n, state_m, state_conv, meta_tokens, ffn1_norm, ffn1_w_gate, ffn1_w_up, ffn1_w_down, mix_norm, w_in, conv_w, conv_b, w_mq, w_mk, w_mv, b_igate, b_fgate, mlstm_norm, lambda_q1, lambda_k1, lambda_q2, lambda_k2, subln, w_proj_a, w_proj_b, w_out, ffn2_norm, ffn2_w_gate, ffn2_w_up, ffn2_w_down, final_norm):
    bp, seq, _ = x_prompt.shape
    bs, ts, _ = x_sample.shape
    n_pages = page_table.shape[1]
    page = cache_k.shape[2]
    past_len = n_pages * page
    assert w_in.shape[0] == 1 and ts <= QSUB

    win = w_in[0]
    offs = [0]
    for width in (D_MODEL, D_MODEL, NH_M, NH_M, D_MODEL, D_MODEL, D_MODEL, D_MODEL, D_MODEL):
        offs.append(offs[-1] + width)
    col = lambda i: win[:, offs[i]:offs[i + 1]]
    wgate = jnp.pad(jnp.concatenate([col(2), col(3)], axis=1), ((0, 0), (0, LANES - 2 * NH_M)))
    wproj = {"xm": col(0).astype(BF16), "om": col(1).astype(BF16), "q": col(4).astype(BF16),
             "k": col(5).astype(BF16), "v": col(6).astype(BF16), "ga": col(7).astype(BF16),
             "gb": col(8).astype(BF16), "gate": wgate.astype(BF16)}
    gbias = jnp.pad(jnp.concatenate([b_igate[0], b_fgate[0]])[None, :], ((0, 0), (0, LANES - 2 * NH_M)))
    wm = {"conv_w": conv_w[0], "conv_b": conv_b[0][None, :],
          "mq": w_mq[0].astype(BF16), "mk": (w_mk[0] * DH_M ** -0.5).astype(BF16), "mv": w_mv[0].astype(BF16),
          "mvt": jnp.swapaxes(w_mv[0], 1, 2).astype(BF16), "mnorm": mlstm_norm[0].reshape(1, D_MODEL)}
    lams = (lambda_q1, lambda_k1, lambda_q2, lambda_k2)
    row = lambda a: a.reshape(1, -1)
    f1 = (row(ffn1_norm[0]), ffn1_w_gate[0].astype(BF16), ffn1_w_up[0].astype(BF16), ffn1_w_down[0].astype(BF16))
    f2 = (row(ffn2_norm[0]), ffn2_w_gate[0].astype(BF16), ffn2_w_up[0].astype(BF16), ffn2_w_down[0].astype(BF16))
    fin = row(final_norm)
    wa, wb, wo = w_proj_a[0].astype(BF16), w_proj_b[0].astype(BF16), w_out[0].astype(BF16)

    n_main = bp * seq
    n_samp = bs * ts
    x_main = x_prompt.reshape(n_main, D_MODEL)
    x_small = jnp.concatenate([x_sample.reshape(n_samp, D_MODEL), meta_tokens], axis=0)

    tm_in = min(seq, 512)
    tabs_main = _rope_tables(N_META + jnp.arange(seq, dtype=jnp.int32))
    pos_small = jnp.concatenate([jnp.tile(past_len + jnp.arange(ts, dtype=jnp.int32), bs),
                                 jnp.arange(N_META, dtype=jnp.int32)])
    tabs_small = _rope_tables(pos_small)

    x1_main = _ffn(x_main, *f1, fin, final_norm=False)
    x1_small = _ffn(x_small, *f1, fin, final_norm=False)
    pm = _inproj(x1_main, row(mix_norm[0]), wproj, gbias, tabs_main, n_rope_blocks=seq // tm_in, tm=tm_in)
    ps = _inproj(x1_small, row(mix_norm[0]), wproj, gbias, tabs_small, n_rope_blocks=1, tm=x_small.shape[0])
    xm_m, om_m, q_m, k_m, v_m, ga_m, gb_m, gcol_m, grow_m = pm
    xm_s, om_s, q_s, k_s, v_s, ga_s, gb_s, gcol_s, grow_s = ps

    def split(a):
        return a[:n_samp], a[n_samp:]

    xm_smp, xm_meta = split(xm_s)
    om_smp, om_meta = split(om_s)
    gcol_smp, gcol_meta = split(gcol_s)
    grow_smp, grow_meta = grow_s[:, :n_samp], grow_s[:, n_samp:]

    zc = jnp.zeros((1, NH_M, DH_M, DH_M), F32)
    zn = jnp.zeros((1, SUBLANES, DH_M), F32)
    zm = jnp.zeros((1, SUBLANES, LANES), F32)
    zbuf = jnp.zeros((1, SUBLANES, D_MODEL), F32)
    hm_meta, c_meta, n_meta, m_meta = _mlstm(
        xm_meta[None], om_meta[None], gcol_meta[None], grow_meta.reshape(1, 1, SUBLANES, N_META),
        zbuf, zc, zn, zm, wm, chunk=N_META, nb=1)

    chunk = _row_tile(seq, MLSTM_CHUNK)
    grow_main = grow_m.reshape(SUBLANES, bp, seq // chunk, chunk).transpose(1, 2, 0, 3)
    buf_main = _pad_rows(xm_meta[N_META - (CONV_W - 1):], SUBLANES, 0, front=True)[None]
    hm_main, c_main, n_main_st, m_main = _mlstm(
        xm_m.reshape(bp, seq, D_MODEL), om_m.reshape(bp, seq, D_MODEL), gcol_m.reshape(bp, seq, LANES),
        grow_main, buf_main, c_meta, n_meta, m_meta, wm, chunk=chunk, nb=_group(bp))

    pad_t = lambda a: _pad_rows(a.reshape(bs, ts, a.shape[-1]), QROWS, 1)
    gcol_pad = jnp.concatenate(
        [gcol_smp.reshape(bs, ts, LANES),
         jnp.broadcast_to(jnp.where(jnp.arange(LANES) < NH_M, NEG_BIG, 0.0).astype(F32), (bs, QROWS - ts, LANES))],
        axis=1)
    grow_pad = jnp.concatenate(
        [grow_smp.reshape(SUBLANES, bs, ts),
         jnp.broadcast_to(jnp.where(jnp.arange(SUBLANES) < NH_M, NEG_BIG, 0.0).astype(F32)[:, None, None],
                          (SUBLANES, bs, QROWS - ts))], axis=2).transpose(1, 0, 2)[:, None]
    hm_smp, c_smp, n_smp, m_smp = _mlstm(
        pad_t(xm_smp), pad_t(om_smp), gcol_pad, grow_pad,
        _pad_rows(state_conv[0], SUBLANES, 1, front=True), state_C[0],
        _pad_rows(state_n[0], SUBLANES, 1), jnp.broadcast_to(_pad_rows(state_m[0], SUBLANES, 1)[:, :, None],
                                                             (bs, SUBLANES, LANES)),
        wm, chunk=QROWS, nb=_group(bs))

    q_smp, q_meta = split(q_s)
    k_smp, k_meta = split(k_s)
    v_smp, v_meta = split(v_s)
    sub = row(subln[0])
    sub_col = subln[0].reshape(DV_D, 1)
    o_meta = _attn_prompt(q_meta[None], k_meta[None], v_meta[None], k_meta, v_meta, lams, sub_col,
                          tq=N_META, has_prefix=False)
    tq = min(seq, ATTN_TQ)
    o_main, k_full, v_full = _attn_prompt(q_m.reshape(bp, seq, D_MODEL), k_m.reshape(bp, seq, D_MODEL),
                                          v_m.reshape(bp, seq, D_MODEL), k_meta, v_meta, lams, sub_col,
                                          tq=tq, has_prefix=True)
    del o_meta
    o_smp = _attn_sample(page_table, pad_t(q_smp), cache_k.reshape(-1, page * NH_D, DV_D),
                         cache_v.reshape(-1, page * NH_D, DV_D), k_smp.reshape(bs, ts * NH_D, DV_D),
                         v_smp.reshape(bs, ts * NH_D, DV_D), lams, sub, n_new=ts)
    o_smp = o_smp.reshape(bs, NH_D, QSUB, DV_D)[:, :, :ts].transpose(0, 2, 1, 3).reshape(n_samp, D_MODEL)

    x2_main = _merge(hm_main.reshape(n_main, D_MODEL), o_main.reshape(n_main, D_MODEL), ga_m, gb_m, x1_main,
                     wa, wb, wo)
    y_main = _ffn(x2_main, *f2, fin, final_norm=True)
    x2_smp = _merge(hm_smp[:, :ts].reshape(n_samp, D_MODEL), o_smp,
                    ga_s[:n_samp], gb_s[:n_samp], x1_small[:n_samp], wa, wb, wo)
    y_smp = _ffn(x2_smp, *f2, fin, final_norm=True)
    del hm_meta

    xm_main3 = xm_m.reshape(bp, seq, D_MODEL)
    return (
        y_main.reshape(bp, seq, D_MODEL),
        y_smp.reshape(bs, ts, D_MODEL),
        k_full.reshape(1, bp, seq + N_META, NH_D, DV_D),
        v_full.reshape(1, bp, seq + N_META, NH_D, DV_D),
        c_main[None],
        n_main_st[None, :, :NH_M, :],
        m_main[None, :, :NH_M, 0],
        xm_main3[None, :, seq - (CONV_W - 1):, :],
        k_smp.reshape(1, bs, ts, NH_D, DV_D),
        v_smp.reshape(1, bs, ts, NH_D, DV_D),
        c_smp[None],
        n_smp[None, :, :NH_M, :],
        m_smp[None, :, :NH_M, 0],
        xm_smp.reshape(bs, ts, D_MODEL)[None, :, ts - (CONV_W - 1):, :],
    )
```

```python
import functools
import math

import jax
import jax.numpy as jnp
from jax import lax
from jax.experimental import pallas as pl
from jax.experimental.pallas import tpu as pltpu

F32 = jnp.float32
BF16 = jnp.bfloat16

D_MODEL = 1024
N_META = 16
EPS = 1e-6
D_FF = 2816
NH_M = 4
DH_M = 256
CONV_W = 4
NH_D = 8
DH_D = 64
DV_D = 128
ROT_DIM = 16
ROPE_THETA = 500000.0
LAMBDA_INIT = 0.8 - 0.6 * math.exp(-0.3 * 0)

LANES = 128
SUBLANES = 8
NEG_BIG = -1e30
VMEM_LIMIT = 56 * 1024 * 1024


def _cparams(sem, flags=None):
    return pltpu.CompilerParams(dimension_semantics=sem, vmem_limit_bytes=VMEM_LIMIT, flags=flags)


def _rms(x, g):
    return x * lax.rsqrt(jnp.mean(x * x, axis=-1, keepdims=True) + EPS) * g


def _sigmoid(x):
    return 1.0 / (1.0 + jnp.exp(-x))


def _dot(a, b):
    return jnp.dot(a, b, preferred_element_type=F32)


def _dot_nt(a, b):
    return lax.dot_general(a, b, (((1,), (1,)), ((), ())), preferred_element_type=F32)


def _row_tile(n, want):
    if n <= 2 * want:
        return n
    t = want
    while n % t:
        t //= 2
    assert t == n or t % SUBLANES == 0
    return t


MLSTM_CHUNK = 256
MLSTM_GROUP = 4


def _group(n):
    return max(g for g in range(1, MLSTM_GROUP + 1) if n % g == 0)


def _full(shape):
    zeros = (0,) * len(shape)
    return pl.BlockSpec(shape, lambda *_: zeros)


def _resident(shape):
    zeros = (0,) * len(shape)
    return pl.BlockSpec(shape, lambda *_: zeros, pipeline_mode=pl.Buffered(1))


def _ffn_body(x_ref, g_ref, wg_ref, wu_ref, wd_ref, fg_ref, o_ref, *, final_norm):
    x = x_ref[...]
    xb = _rms(x, g_ref[...]).astype(BF16)
    g = _dot(xb, wg_ref[...])
    u = _dot(xb, wu_ref[...])
    a = (g * _sigmoid(g) * u).astype(BF16)
    y = x + 0.5 * _dot(a, wd_ref[...])
    if final_norm:
        y = _rms(y, fg_ref[...])
    o_ref[...] = y


def _ffn(x, gain, wg, wu, wd, final_gain, *, final_norm, tm=1024):
    n = x.shape[0]
    tm = _row_tile(n, tm)
    row = pl.BlockSpec((tm, D_MODEL), lambda i: (i, 0))
    return pl.pallas_call(
        functools.partial(_ffn_body, final_norm=final_norm),
        grid=(n // tm,),
        in_specs=[row, _full((1, D_MODEL)), _resident(wg.shape), _resident(wu.shape), _resident(wd.shape),
                  _full((1, D_MODEL))],
        out_specs=row,
        out_shape=jax.ShapeDtypeStruct((n, D_MODEL), F32),
        compiler_params=_cparams(("parallel",)),
        name="ffn_final" if final_norm else "ffn",
    )(x, gain, wg, wu, wd, final_gain)


def _rope_slab(xs, c, s_up, s_dn):
    return xs * c + pltpu.roll(xs, LANES - ROT_DIM // 2, axis=1) * s_up + pltpu.roll(xs, ROT_DIM // 2, axis=1) * s_dn


def _inproj_body(x_ref, g_ref, wxm_ref, wom_ref, wq_ref, wk_ref, wv_ref, wga_ref, wgb_ref, wgate_ref,
                 gbias_ref, cq_ref, cu_ref, cd_ref,
                 xm_ref, om_ref, q_ref, k_ref, v_ref, ga_ref, gb_ref, gcol_ref, grow_ref):
    hb = _rms(x_ref[...], g_ref[...]).astype(BF16)
    xm_ref[...] = _dot(hb, wxm_ref[...])
    om_ref[...] = _sigmoid(_dot(hb, wom_ref[...]))
    v_ref[...] = _dot(hb, wv_ref[...])
    ga_ref[...] = _dot(hb, wga_ref[...])
    gb_ref[...] = _dot(hb, wgb_ref[...])
    c, s_up, s_dn = cq_ref[...], cu_ref[...], cd_ref[...]
    q = _dot(hb, wq_ref[...])
    k = _dot(hb, wk_ref[...])
    qscale = DH_D ** -0.5 * math.log2(math.e)
    for h in range(NH_D):
        sl = slice(h * LANES, (h + 1) * LANES)
        q_ref[:, sl] = (_rope_slab(q[:, sl], c, s_up, s_dn) * qscale).astype(BF16)
        k_ref[:, sl] = _rope_slab(k[:, sl], c, s_up, s_dn)
    pre = _dot(hb, wgate_ref[...]) + gbias_ref[...]
    lane = lax.broadcasted_iota(jnp.int32, pre.shape, 1)
    logsig = jnp.minimum(pre, 0.0) - jnp.log(1.0 + jnp.exp(-jnp.abs(pre)))
    glog = jnp.where(lane < NH_M, pre, logsig)
    gcol_ref[...] = glog
    grow_ref[...] = glog.T[:SUBLANES, :]


def _inproj(x, gain, w, gbias, rope_tabs, *, n_rope_blocks, tm):
    n = x.shape[0]
    assert n % tm == 0
    row = pl.BlockSpec((tm, D_MODEL), lambda i: (i, 0))
    tab = pl.BlockSpec((tm, LANES), lambda i: (i % n_rope_blocks, 0))
    wspec = _resident((D_MODEL, D_MODEL))
    out_shapes = (
        jax.ShapeDtypeStruct((n, D_MODEL), F32),
        jax.ShapeDtypeStruct((n, D_MODEL), F32),
        jax.ShapeDtypeStruct((n, D_MODEL), BF16),
        jax.ShapeDtypeStruct((n, D_MODEL), F32),
        jax.ShapeDtypeStruct((n, D_MODEL), F32),
        jax.ShapeDtypeStruct((n, D_MODEL), F32),
        jax.ShapeDtypeStruct((n, D_MODEL), F32),
        jax.ShapeDtypeStruct((n, LANES), F32),
        jax.ShapeDtypeStruct((SUBLANES, n), F32),
    )
    return pl.pallas_call(
        _inproj_body,
        grid=(n // tm,),
        in_specs=[row, _full((1, D_MODEL))] + [wspec] * 7 + [_full((D_MODEL, LANES)), _full((1, LANES)),
                                                              tab, tab, tab],
        out_specs=(row, row, row, row, row, row, row,
                   pl.BlockSpec((tm, LANES), lambda i: (i, 0)),
                   pl.BlockSpec((SUBLANES, tm), lambda i: (0, i))),
        out_shape=out_shapes,
        compiler_params=_cparams(("parallel",)),
        name="inproj",
    )(x, gain, w["xm"], w["om"], w["q"], w["k"], w["v"], w["ga"], w["gb"], w["gate"], gbias, *rope_tabs)


def _mlstm_body(xm_ref, xprev_ref, buf0_ref, om_ref, gcol_ref, grow_ref, cw_ref, cb_ref,
                wq_ref, wk_ref, wv_ref, wvt_ref, norm_ref, c0_ref, n0_ref, m0_ref,
                hm_ref, c_ref, n_ref, m_ref, ext_ref, *, chunk, shared_init):
    ci = pl.program_id(1)
    L = chunk
    nb = xm_ref.shape[0]

    @pl.when(ci == 0)
    def _():
        for bi in range(nb):
            src = 0 if shared_init else bi
            c_ref[bi] = c0_ref[src]
            n_ref[bi] = n0_ref[src]
            m_ref[bi] = m0_ref[src]
            ext_ref[bi, 0:SUBLANES, :] = buf0_ref[src]

    @pl.when(ci > 0)
    def _():
        ext_ref[:, 0:SUBLANES, :] = xprev_ref[...]

    x = xm_ref[...]
    ext_ref[:, SUBLANES:SUBLANES + L, :] = x
    cv = cb_ref[...]
    for j in range(CONV_W):
        cv = cv + ext_ref[:, pl.ds(SUBLANES - (CONV_W - 1) + j, L), :] * cw_ref[j:j + 1, :]
    cact = cv * _sigmoid(cv)

    ti = lax.broadcasted_iota(jnp.int32, (L, L), 0)
    si = lax.broadcasted_iota(jnp.int32, (L, L), 1)
    tril = si <= ti

    for h in range(NH_M):
        hs = slice(h * DH_M, (h + 1) * DH_M)
        chb = cact[:, :, hs].reshape(nb * L, DH_M).astype(BF16)
        xhb = x[:, :, hs].reshape(nb * L, DH_M).astype(BF16)
        q_all = _dot(chb, wq_ref[h])
        k_all = _dot(chb, wk_ref[h])
        v_all = _dot(xhb, wv_ref[h])
        for bi in range(nb):
            rows = slice(bi * L, (bi + 1) * L)
            q, k, v = q_all[rows], k_all[rows], v_all[rows]
            vt = _dot_nt(wvt_ref[h], xhb[rows])
            qb, kb, vb = q.astype(BF16), k.astype(BF16), v.astype(BF16)

            gcol = gcol_ref[bi]
            grow = grow_ref[bi, 0]
            logi_c, logf_c = gcol[:, h:h + 1], gcol[:, NH_M + h:NH_M + h + 1]
            logi_r, logf_r = grow[h:h + 1, :], grow[NH_M + h:NH_M + h + 1, :]
            b_c = jnp.sum(jnp.where(tril, logf_r, 0.0), axis=1, keepdims=True)
            b_r = jnp.sum(jnp.where(ti <= si, logf_c, 0.0), axis=0, keepdims=True)
            m_prev = m_ref[bi, h:h + 1, 0:1]
            n_prev = n_ref[bi, h:h + 1, :]
            c_prev = c_ref[bi, h]

            d = jnp.where(tril, b_c - b_r + logi_r, -jnp.inf)
            inter = b_c + m_prev
            m_t = jnp.maximum(jnp.max(d, axis=1, keepdims=True), inter)
            w_intra = jnp.exp(d - m_t)
            w_inter = jnp.exp(inter - m_t)
            s = _dot_nt(qb, kb) * w_intra
            num = _dot(s.astype(BF16), vb) + w_inter * _dot_nt(qb, c_prev.astype(BF16))
            den = jnp.sum(s, axis=1, keepdims=True) + w_inter * jnp.sum(q * n_prev, axis=1, keepdims=True)
            hh = num / jnp.maximum(jnp.abs(den), jnp.exp(-m_t))

            m_new = m_t[L - 1:L, :]
            b_last = b_c[L - 1:L, :]
            w_state_r = jnp.exp(b_last - b_r + logi_r - m_new)
            w_state_c = jnp.exp(b_last - b_c + logi_c - m_new)
            decay = jnp.exp(b_last + m_prev - m_new)
            c_ref[bi, h] = decay * c_prev + _dot((vt * w_state_r).astype(BF16), kb)
            n_ref[bi, h:h + 1, :] = decay * n_prev + jnp.sum(k * w_state_c, axis=0, keepdims=True)
            m_ref[bi, h:h + 1, :] = jnp.broadcast_to(m_new, (1, LANES))

            hn = hh * lax.rsqrt(jnp.mean(hh * hh, axis=1, keepdims=True) + EPS) * norm_ref[:, hs]
            hm_ref[bi, :, hs] = (hn * om_ref[bi, :, hs]).astype(BF16)


def _mlstm(xm, om, gcol, grow, buf0, c0, n0, m0, w, *, chunk, nb):
    B, T, _ = xm.shape
    L = chunk
    nc = T // L
    shared = c0.shape[0] == 1 and B > 1
    nb0 = 1 if shared else nb
    st = (lambda b, c: (0, 0, 0)) if shared else (lambda b, c: (b, 0, 0))
    st4 = (lambda b, c: (0, 0, 0, 0)) if shared else (lambda b, c: (b, 0, 0, 0))
    seq = pl.BlockSpec((nb, L, D_MODEL), lambda b, c: (b, c, 0))
    wsq = _full((NH_M, DH_M, DH_M))
    return pl.pallas_call(
        functools.partial(_mlstm_body, chunk=L, shared_init=shared),
        grid=(B // nb, nc),
        in_specs=[
            seq,
            pl.BlockSpec((nb, SUBLANES, D_MODEL), lambda b, c: (b, jnp.maximum(c * (L // SUBLANES) - 1, 0), 0)),
            pl.BlockSpec((nb0, SUBLANES, D_MODEL), st),
            seq,
            pl.BlockSpec((nb, L, LANES), lambda b, c: (b, c, 0)),
            pl.BlockSpec((nb, 1, SUBLANES, L), lambda b, c: (b, c, 0, 0)),
            _full((CONV_W, D_MODEL)), _full((1, D_MODEL)),
            wsq, wsq, wsq, wsq, _full((1, D_MODEL)),
            pl.BlockSpec((nb0, NH_M, DH_M, DH_M), st4),
            pl.BlockSpec((nb0, SUBLANES, DH_M), st),
            pl.BlockSpec((nb0, SUBLANES, LANES), st),
        ],
        out_specs=(
            seq,
            pl.BlockSpec((nb, NH_M, DH_M, DH_M), lambda b, c: (b, 0, 0, 0)),
            pl.BlockSpec((nb, SUBLANES, DH_M), lambda b, c: (b, 0, 0)),
            pl.BlockSpec((nb, SUBLANES, LANES), lambda b, c: (b, 0, 0)),
        ),
        out_shape=(
            jax.ShapeDtypeStruct((B, T, D_MODEL), BF16),
            jax.ShapeDtypeStruct((B, NH_M, DH_M, DH_M), F32),
            jax.ShapeDtypeStruct((B, SUBLANES, DH_M), F32),
            jax.ShapeDtypeStruct((B, SUBLANES, LANES), F32),
        ),
        scratch_shapes=[pltpu.VMEM((nb, SUBLANES + L, D_MODEL), F32)],
        compiler_params=_cparams(("parallel", "arbitrary")),
        name="mlstm",
    )(xm, xm, buf0, om, gcol, grow, w["conv_w"], w["conv_b"], w["mq"], w["mk"], w["mv"], w["mvt"],
      w["mnorm"], c0, n0, m0)


def _lam(lq1_ref, lk1_ref, lq2_ref, lk2_ref):
    return (jnp.exp(jnp.sum(lq1_ref[...] * lk1_ref[...], axis=1, keepdims=True))
            - jnp.exp(jnp.sum(lq2_ref[...] * lk2_ref[...], axis=1, keepdims=True)) + LAMBDA_INIT)


def _online_softmax_step(m_ref, l_ref, acc_ref, s, vb):
    m_old = m_ref[...]
    m_new = jnp.maximum(m_old, jnp.max(s, axis=1, keepdims=True))
    pr = jnp.exp2(s - m_new)
    r = jnp.exp2(m_old - m_new)
    l_ref[...] = r * l_ref[...] + jnp.sum(pr, axis=1, keepdims=True)
    acc_ref[...] = r * acc_ref[...] + _dot(pr.astype(BF16), vb)
    m_ref[...] = m_new


ATTN_TQ = 1024
ONES_ROWS = 16


def _attn_prompt_body(q_ref, k_ref, v_ref, kp_ref, vp_ref, vpt_ref, lq1_ref, lk1_ref, lq2_ref, lk2_ref, subln_ref,
                      o_ref, kf_ref, vf_ref, kb_ref, vt_ref, m_ref, acc_ref, sa_ref, sb_ref, ma_ref, mb_ref, *, tq, tk,
                      has_prefix):
    qi = pl.program_id(2)
    nblk = vt_ref.shape[0]

    def with_ones(vt):
        keys = vt.shape[1]
        row = lax.broadcasted_iota(jnp.int32, (ONES_ROWS, keys), 0)
        return jnp.concatenate([vt, jnp.where(row == 0, 1.0, 0.0)], axis=0).astype(BF16)

    @pl.when(qi == 0)
    def _():
        npre = kp_ref.shape[0]
        kf_ref[0, 0:npre, :] = kp_ref[...]
        vf_ref[0, 0:npre, :] = vp_ref[...]
        kf_ref[0, npre:, :] = k_ref[0]
        vf_ref[0, npre:, :] = v_ref[0]
        kb_ref[...] = k_ref[0].astype(BF16)
        for c in range(nblk):
            vt_ref[c] = with_ones(v_ref[0, c * tk:(c + 1) * tk, :].T)

    q = q_ref[0].astype(F32)
    lane = lax.broadcasted_iota(jnp.int32, q.shape, 1)
    qq = jnp.concatenate([jnp.where(lane < DH_D, q, 0.0), jnp.where(lane < DH_D, 0.0, q)], axis=0)
    qqt = qq.T.astype(BF16)
    m_ref[...] = jnp.full(m_ref.shape, -jnp.inf, F32)
    acc_ref[...] = jnp.zeros(acc_ref.shape, F32)

    def apply(st, mx, vt):
        m_old = m_ref[...]
        m_new = jnp.maximum(m_old, mx)
        acc_ref[...] = jnp.exp2(m_old - m_new) * acc_ref[...] + _dot(vt, jnp.exp2(st - m_new).astype(BF16))
        m_ref[...] = m_new

    def step(buf, j):
        s_ref, mx_ref = buf
        apply(s_ref[...], mx_ref[...], vt_ref[j])

    rel = (lax.broadcasted_iota(jnp.int32, (tk, 2 * tq), 0)
           - lax.broadcasted_iota(jnp.int32, (tk, 2 * tq), 1) % tq)

    def put(buf, j, masked):
        s_ref, mx_ref = buf
        st = _dot(kb_ref[pl.ds(pl.multiple_of(j * tk, tk), tk), :], qqt)
        if masked:
            st = jnp.where(rel <= qi * tq - j * tk, st, -jnp.inf)
        s_ref[...] = st
        mx_ref[...] = jnp.max(st, axis=0, keepdims=True)

    if has_prefix:
        st = _dot(kp_ref[...].astype(BF16), qqt)
        apply(st, jnp.max(st, axis=0, keepdims=True), with_ones(vpt_ref[...]))

    assert tq == 2 * tk
    buf_a, buf_b = (sa_ref, ma_ref), (sb_ref, mb_ref)
    put(buf_a, 0, True)

    def pair(p, carry):
        j = 2 * p
        put(buf_b, j + 1, False)
        step(buf_a, j)
        put(buf_a, j + 2, False)
        step(buf_b, j + 1)
        return carry

    lax.fori_loop(0, qi - 1, pair, 0)

    @pl.when(qi >= 1)
    def _():
        j = 2 * (qi - 1)
        put(buf_b, j + 1, False)
        step(buf_a, j)
        put(buf_a, j + 2, True)
        step(buf_b, j + 1)

    half = tq // 2
    cols = (slice(half, tq), slice(tq + half, 2 * tq))
    last = 2 * qi + 1
    qh = jnp.concatenate([qqt[:, c] for c in cols], axis=1)
    st = _dot(kb_ref[pl.ds(pl.multiple_of(last * tk, tk), tk), :], qh)
    tri = (lax.broadcasted_iota(jnp.int32, (tk, tq), 0)
           <= lax.broadcasted_iota(jnp.int32, (tk, tq), 1) % half)
    st = jnp.where(tri, st, -jnp.inf)
    mx = jnp.max(st, axis=0, keepdims=True)
    step(buf_a, 2 * qi)
    m_old = jnp.concatenate([m_ref[:, c] for c in cols], axis=1)
    m_new = jnp.maximum(m_old, mx)
    r = jnp.exp2(m_old - m_new)
    pv = _dot(vt_ref[last], jnp.exp2(st - m_new).astype(BF16))
    for i, c in enumerate(cols):
        part = slice(i * half, (i + 1) * half)
        acc_ref[:, c] = r[:, part] * acc_ref[:, c] + pv[:, part]
        m_ref[:, c] = m_new[:, part]

    lam = _lam(lq1_ref, lk1_ref, lq2_ref, lk2_ref)
    acc = acc_ref[0:DV_D, :] / acc_ref[DV_D:DV_D + 1, :]
    ot = acc[:, 0:tq] - lam * acc[:, tq:2 * tq]
    ot = ot * lax.rsqrt(jnp.mean(ot * ot, axis=0, keepdims=True) + EPS) * subln_ref[...]
    o_ref[0] = (ot * (1.0 - LAMBDA_INIT)).T.astype(BF16)


def _attn_prompt(q, k, v, kpre, vpre, lams, subln_col, *, tq, has_prefix):
    vpre_t = vpre.T
    B, T, _ = q.shape
    P = kpre.shape[0]
    tk = tq // 2
    assert T % tq == 0 and tq % (2 * tk) == 0
    lspec = _full((1, DH_D))
    return pl.pallas_call(
        functools.partial(_attn_prompt_body, tq=tq, tk=tk, has_prefix=has_prefix),
        grid=(B, NH_D, T // tq),
        in_specs=[
            pl.BlockSpec((1, tq, LANES), lambda b, h, i: (b, i, h)),
            pl.BlockSpec((1, T, LANES), lambda b, h, i: (b, 0, h)),
            pl.BlockSpec((1, T, LANES), lambda b, h, i: (b, 0, h)),
            pl.BlockSpec((P, LANES), lambda b, h, i: (0, h)),
            pl.BlockSpec((P, LANES), lambda b, h, i: (0, h)),
            pl.BlockSpec((DV_D, P), lambda b, h, i: (h, 0)),
            lspec, lspec, lspec, lspec, _full((DV_D, 1)),
        ],
        out_specs=(pl.BlockSpec((1, tq, LANES), lambda b, h, i: (b, i, h)),
                   pl.BlockSpec((1, P + T, LANES), lambda b, h, i: (b, 0, h)),
                   pl.BlockSpec((1, P + T, LANES), lambda b, h, i: (b, 0, h))),
        out_shape=(jax.ShapeDtypeStruct((B, T, D_MODEL), BF16),
                   jax.ShapeDtypeStruct((B, P + T, D_MODEL), F32),
                   jax.ShapeDtypeStruct((B, P + T, D_MODEL), F32)),
        scratch_shapes=[pltpu.VMEM((T, LANES), BF16), pltpu.VMEM((T // tk, DV_D + ONES_ROWS, tk), BF16),
                        pltpu.VMEM((1, 2 * tq), F32), pltpu.VMEM((DV_D + ONES_ROWS, 2 * tq), F32),
                        pltpu.VMEM((tk, 2 * tq), F32), pltpu.VMEM((tk, 2 * tq), F32),
                        pltpu.VMEM((1, 2 * tq), F32), pltpu.VMEM((1, 2 * tq), F32)],
        compiler_params=_cparams(("parallel", "parallel", "arbitrary")),
        name="attn_prompt",
    )(q, k, v, kpre, vpre, vpre_t, *lams, subln_col)


QROWS = 8
QSUB = 4
QMAT_ROWS = 2 * NH_D * QSUB
PAGES_PER_STEP = 16


def _attn_sample_body(pt_ref, q_ref, *refs, n_new, n_pg):
    kc_refs, vc_refs = refs[:n_pg], refs[n_pg:2 * n_pg]
    (kn_ref, vn_ref, lq1_ref, lk1_ref, lq2_ref, lk2_ref, subln_ref,
     o_ref, qm_ref, bias_ref, m_ref, l_ref, acc_ref) = refs[2 * n_pg:]
    p = pl.program_id(1)
    R = QMAT_ROWS
    rows = kc_refs[0].shape[1]
    half = R // 2

    def head_of(r):
        return (r % half) // QSUB

    @pl.when(p == 0)
    def _():
        q = q_ref[0].astype(F32)
        sub = lax.broadcasted_iota(jnp.int32, (QROWS, LANES), 0)
        pieces = []
        for j in range(NH_D // 2):
            even = q[:, (2 * j) * LANES:(2 * j + 1) * LANES]
            odd = q[:, (2 * j + 1) * LANES:(2 * j + 2) * LANES]
            pieces.append(jnp.where(sub < QSUB, even, pltpu.roll(odd, QSUB, axis=0)))
        qh = jnp.concatenate(pieces, axis=0)
        lane = lax.broadcasted_iota(jnp.int32, qh.shape, 1)
        qm_ref[...] = jnp.concatenate([jnp.where(lane < DH_D, qh, 0.0), jnp.where(lane < DH_D, 0.0, qh)],
                                      axis=0).astype(BF16)
        row = lax.broadcasted_iota(jnp.int32, (R, rows), 0)
        col = lax.broadcasted_iota(jnp.int32, (R, rows), 1)
        bias_ref[...] = jnp.where(col % NH_D == head_of(row), 0.0, -jnp.inf)
        m_ref[...] = jnp.full(m_ref.shape, -jnp.inf, F32)
        l_ref[...] = jnp.zeros(l_ref.shape, F32)
        acc_ref[...] = jnp.zeros(acc_ref.shape, F32)

    qm = qm_ref[...]
    bias = bias_ref[...]
    s = [_dot_nt(qm, kc[0].astype(BF16)) + bias for kc in kc_refs]
    m_old = m_ref[...]
    m_new = m_old
    for si in s:
        m_new = jnp.maximum(m_new, jnp.max(si, axis=1, keepdims=True))
    r = jnp.exp2(m_old - m_new)
    l_new = r * l_ref[...]
    acc = r * acc_ref[...]
    for si, vc in zip(s, vc_refs):
        pr = jnp.exp2(si - m_new)
        l_new = l_new + jnp.sum(pr, axis=1, keepdims=True)
        acc = acc + _dot(pr.astype(BF16), vc[0].astype(BF16))
    m_ref[...] = m_new
    l_ref[...] = l_new
    acc_ref[...] = acc

    @pl.when(p == pl.num_programs(1) - 1)
    def _():
        zpad = jnp.zeros((LANES - kn_ref.shape[1], LANES), F32)
        kn = jnp.concatenate([kn_ref[0], zpad], axis=0).astype(BF16)
        vn = jnp.concatenate([vn_ref[0], zpad], axis=0).astype(BF16)
        sn = _dot_nt(qm, kn)
        row = lax.broadcasted_iota(jnp.int32, sn.shape, 0)
        col = lax.broadcasted_iota(jnp.int32, sn.shape, 1)
        tok = col // NH_D
        ok = (col % NH_D == head_of(row)) & (tok <= row % QSUB) & (tok < n_new)
        _online_softmax_step(m_ref, l_ref, acc_ref, jnp.where(ok, sn, -jnp.inf), vn)
        lam = _lam(lq1_ref, lk1_ref, lq2_ref, lk2_ref)
        o = acc_ref[0:half, :] / l_ref[0:half, :] - lam * (acc_ref[half:R, :] / l_ref[half:R, :])
        o_ref[0] = (_rms(o, subln_ref[...]) * (1.0 - LAMBDA_INIT)).astype(BF16)


def _attn_sample(page_table, q, cache_k, cache_v, knew, vnew, lams, subln, *, n_new):
    B, n_pages = page_table.shape
    rows = cache_k.shape[1]
    R = QMAT_ROWS
    n_pg = math.gcd(n_pages, PAGES_PER_STEP)
    lspec = pl.BlockSpec((1, DH_D), lambda b, p, pt: (0, 0))
    pgs = [pl.BlockSpec((1, rows, LANES), lambda b, p, pt, i=i: (pt[b, p * n_pg + i], 0, 0)) for i in range(n_pg)]
    new = pl.BlockSpec((1, n_new * NH_D, LANES), lambda b, p, pt: (b, 0, 0))
    return pl.pallas_call(
        functools.partial(_attn_sample_body, n_new=n_new, n_pg=n_pg),
        grid_spec=pltpu.PrefetchScalarGridSpec(
            num_scalar_prefetch=1,
            grid=(B, n_pages // n_pg),
            in_specs=[pl.BlockSpec((1, QROWS, D_MODEL), lambda b, p, pt: (b, 0, 0))] + pgs + pgs
                     + [new, new, lspec, lspec, lspec, lspec, pl.BlockSpec((1, DV_D), lambda b, p, pt: (0, 0))],
            out_specs=pl.BlockSpec((1, R // 2, DV_D), lambda b, p, pt: (b, 0, 0)),
            scratch_shapes=[pltpu.VMEM((R, LANES), BF16), pltpu.VMEM((R, rows), F32), pltpu.VMEM((R, 1), F32),
                            pltpu.VMEM((R, 1), F32), pltpu.VMEM((R, DV_D), F32)],
        ),
        out_shape=jax.ShapeDtypeStruct((B, R // 2, DV_D), BF16),
        compiler_params=_cparams(("parallel", "arbitrary")),
        name="attn_sample",
    )(page_table, q, *([cache_k] * n_pg), *([cache_v] * n_pg), knew, vnew, *lams, subln)


def _merge_body(hm_ref, o_ref, ga_ref, gb_ref, x_ref, wa_ref, wb_ref, wo_ref, y_ref):
    ya = _dot(hm_ref[...], wa_ref[...])
    yb = _dot(o_ref[...], wb_ref[...])
    merged = _sigmoid(ga_ref[...]) * ya + _sigmoid(gb_ref[...]) * yb
    y_ref[...] = x_ref[...] + _dot(merged.astype(BF16), wo_ref[...])


def _merge(hm, o, ga, gb, x, wa, wb, wo, *, tm=1024):
    n = x.shape[0]
    tm = _row_tile(n, tm)
    row = pl.BlockSpec((tm, D_MODEL), lambda i: (i, 0))
    wspec = _resident((D_MODEL, D_MODEL))
    return pl.pallas_call(
        _merge_body,
        grid=(n // tm,),
        in_specs=[row, row, row, row, row, wspec, wspec, wspec],
        out_specs=row,
        out_shape=jax.ShapeDtypeStruct((n, D_MODEL), F32),
        compiler_params=_cparams(("parallel",)),
        name="merge",
    )(hm, o, ga, gb, x, wa, wb, wo)


def _rope_tables(pos):
    half = ROT_DIM // 2
    inv_freq = ROPE_THETA ** (-jnp.arange(half, dtype=F32) / half)
    d = jnp.arange(LANES, dtype=jnp.int32) % DH_D
    freq = jnp.where(d < ROT_DIM, inv_freq[d % half], 0.0)
    ang = pos.astype(F32)[:, None] * freq[None, :]
    c, s = jnp.cos(ang), jnp.sin(ang)
    s_up = jnp.where(d < half, -s, 0.0)
    s_dn = jnp.where(d >= half, s, 0.0)
    return c, s_up, s_dn


def _pad_rows(a, rows, axis, front=False):
    pad = [(0, 0)] * a.ndim
    pad[axis] = (rows - a.shape[axis], 0) if front else (0, rows - a.shape[axis])
    return jnp.pad(a, pad)


def kernel(x_prompt, x_sample, cache_k, cache_v, page_table, state_C, state_n, state_m, state_conv, meta_tokens, ffn1_norm, ffn1_w_gate, ffn1_w_up, ffn1_w_down, mix_norm, w_in, conv_w, conv_b, w_mq, w_mk, w_mv, b_igate, b_fgate, mlstm_norm, lambda_q1, lambda_k1, lambda_q2, lambda_k2, subln, w_proj_a, w_proj_b, w_out, ffn2_norm, ffn2_w_gate, ffn2_w_up, ffn2_w_down, final_norm):
    bp, seq, _ = x_prompt.shape
    bs, ts, _ = x_sample.shape
    n_pages = page_table.shape[1]
    page = cache_k.shape[2]
    past_len = n_pages * page
    assert w_in.shape[0] == 1 and ts <= QSUB

    win = w_in[0]
    offs = [0]
    for width in (D_MODEL, D_MODEL, NH_M, NH_M, D_MODEL, D_MODEL, D_MODEL, D_MODEL, D_MODEL):
        offs.append(offs[-1] + width)
    col = lambda i: win[:, offs[i]:offs[i + 1]]
    wgate = jnp.pad(jnp.concatenate([col(2), col(3)], axis=1), ((0, 0), (0, LANES - 2 * NH_M)))
    wproj = {"xm": col(0).astype(BF16), "om": col(1).astype(BF16), "q": col(4).astype(BF16),
             "k": col(5).astype(BF16), "v": col(6).astype(BF16), "ga": col(7).astype(BF16),
             "gb": col(8).astype(BF16), "gate": wgate.astype(BF16)}
    gbias = jnp.pad(jnp.concatenate([b_igate[0], b_fgate[0]])[None, :], ((0, 0), (0, LANES - 2 * NH_M)))
    wm = {"conv_w": conv_w[0], "conv_b": conv_b[0][None, :],
          "mq": w_mq[0].astype(BF16), "mk": (w_mk[0] * DH_M ** -0.5).astype(BF16), "mv": w_mv[0].astype(BF16),
          "mvt": jnp.swapaxes(w_mv[0], 1, 2).astype(BF16), "mnorm": mlstm_norm[0].reshape(1, D_MODEL)}
    lams = (lambda_q1, lambda_k1, lambda_q2, lambda_k2)
    row = lambda a: a.reshape(1, -1)
    f1 = (row(ffn1_norm[0]), ffn1_w_gate[0].astype(BF16), ffn1_w_up[0].astype(BF16), ffn1_w_down[0].astype(BF16))
    f2 = (row(ffn2_norm[0]), ffn2_w_gate[0].astype(BF16), ffn2_w_up[0].astype(BF16), ffn2_w_down[0].astype(BF16))
    fin = row(final_norm)
    wa, wb, wo = w_proj_a[0].astype(BF16), w_proj_b[0].astype(BF16), w_out[0].astype(BF16)

    n_main = bp * seq
    n_samp = bs * ts
    x_main = x_prompt.reshape(n_main, D_MODEL)
    x_small = jnp.concatenate([x_sample.reshape(n_samp, D_MODEL), meta_tokens], axis=0)

    tm_in = min(seq, 512)
    tabs_main = _rope_tables(N_META + jnp.arange(seq, dtype=jnp.int32))
    pos_small = jnp.concatenate([jnp.tile(past_len + jnp.arange(ts, dtype=jnp.int32), bs),
                                 jnp.arange(N_META, dtype=jnp.int32)])
    tabs_small = _rope_tables(pos_small)

    x1_main = _ffn(x_main, *f1, fin, final_norm=False)
    x1_small = _ffn(x_small, *f1, fin, final_norm=False)
    pm = _inproj(x1_main, row(mix_norm[0]), wproj, gbias, tabs_main, n_rope_blocks=seq // tm_in, tm=tm_in)
    ps = _inproj(x1_small, row(mix_norm[0]), wproj, gbias, tabs_small, n_rope_blocks=1, tm=x_small.shape[0])
    xm_m, om_m, q_m, k_m, v_m, ga_m, gb_m, gcol_m, grow_m = pm
    xm_s, om_s, q_s, k_s, v_s, ga_s, gb_s, gcol_s, grow_s = ps

    def split(a):
        return a[:n_samp], a[n_samp:]

    xm_smp, xm_meta = split(xm_s)
    om_smp, om_meta = split(om_s)
    gcol_smp, gcol_meta = split(gcol_s)
    grow_smp, grow_meta = grow_s[:, :n_samp], grow_s[:, n_samp:]

    zc = jnp.zeros((1, NH_M, DH_M, DH_M), F32)
    zn = jnp.zeros((1, SUBLANES, DH_M), F32)
    zm = jnp.zeros((1, SUBLANES, LANES), F32)
    zbuf = jnp.zeros((1, SUBLANES, D_MODEL), F32)
    hm_meta, c_meta, n_meta, m_meta = _mlstm(
        xm_meta[None], om_meta[None], gcol_meta[None], grow_meta.reshape(1, 1, SUBLANES, N_META),
        zbuf, zc, zn, zm, wm, chunk=N_META, nb=1)

    chunk = _row_tile(seq, MLSTM_CHUNK)
    grow_main = grow_m.reshape(SUBLANES, bp, seq // chunk, chunk).transpose(1, 2, 0, 3)
    buf_main = _pad_rows(xm_meta[N_META - (CONV_W - 1):], SUBLANES, 0, front=True)[None]
    hm_main, c_main, n_main_st, m_main = _mlstm(
        xm_m.reshape(bp, seq, D_MODEL), om_m.reshape(bp, seq, D_MODEL), gcol_m.reshape(bp, seq, LANES),
        grow_main, buf_main, c_meta, n_meta, m_meta, wm, chunk=chunk, nb=_group(bp))

    pad_t = lambda a: _pad_rows(a.reshape(bs, ts, a.shape[-1]), QROWS, 1)
    gcol_pad = jnp.concatenate(
        [gcol_smp.reshape(bs, ts, LANES),
         jnp.broadcast_to(jnp.where(jnp.arange(LANES) < NH_M, NEG_BIG, 0.0).astype(F32), (bs, QROWS - ts, LANES))],
        axis=1)
    grow_pad = jnp.concatenate(
        [grow_smp.reshape(SUBLANES, bs, ts),
         jnp.broadcast_to(jnp.where(jnp.arange(SUBLANES) < NH_M, NEG_BIG, 0.0).astype(F32)[:, None, None],
                          (SUBLANES, bs, QROWS - ts))], axis=2).transpose(1, 0, 2)[:, None]
    hm_smp, c_smp, n_smp, m_smp = _mlstm(
        pad_t(xm_smp), pad_t(om_smp), gcol_pad, grow_pad,
        _pad_rows(state_conv[0], SUBLANES, 1, front=True), state_C[0],
        _pad_rows(state_n[0], SUBLANES, 1), jnp.broadcast_to(_pad_rows(state_m[0], SUBLANES, 1)[:, :, None],
                                                             (bs, SUBLANES, LANES)),
        wm, chunk=QROWS, nb=_group(bs))

    q_smp, q_meta = split(q_s)
    k_smp, k_meta = split(k_s)
    v_smp, v_meta = split(v_s)
    sub = row(subln[0])
    sub_col = subln[0].reshape(DV_D, 1)
    o_meta = _attn_prompt(q_meta[None], k_meta[None], v_meta[None], k_meta, v_meta, lams, sub_col,
                          tq=N_META, has_prefix=False)
    tq = min(seq, ATTN_TQ)
    o_main, k_full, v_full = _attn_prompt(q_m.reshape(bp, seq, D_MODEL), k_m.reshape(bp, seq, D_MODEL),
                                          v_m.reshape(bp, seq, D_MODEL), k_meta, v_meta, lams, sub_col,
                                          tq=tq, has_prefix=True)
    del o_meta
    o_smp = _attn_sample(page_table, pad_t(q_smp), cache_k.reshape(-1, page * NH_D, DV_D),
                         cache_v.reshape(-1, page * NH_D, DV_D), k_smp.reshape(bs, ts * NH_D, DV_D),
                         v_smp.reshape(bs, ts * NH_D, DV_D), lams, sub, n_new=ts)
    o_smp = o_smp.reshape(bs, NH_D, QSUB, DV_D)[:, :, :ts].transpose(0, 2, 1, 3).reshape(n_samp, D_MODEL)

    x2_main = _merge(hm_main.reshape(n_main, D_MODEL), o_main.reshape(n_main, D_MODEL), ga_m, gb_m, x1_main,
                     wa, wb, wo)
    y_main = _ffn(x2_main, *f2, fin, final_norm=True)
    x2_smp = _merge(hm_smp[:, :ts].reshape(n_samp, D_MODEL), o_smp,
                    ga_s[:n_samp], gb_s[:n_samp], x1_small[:n_samp], wa, wb, wo)
    y_smp = _ffn(x2_smp, *f2, fin, final_norm=True)
    del hm_meta

    xm_main3 = xm_m.reshape(bp, seq, D_MODEL)
    return (
        y_main.reshape(bp, seq, D_MODEL),
        y_smp.reshape(bs, ts, D_MODEL),
        k_full.reshape(1, bp, seq + N_META, NH_D, DV_D),
        v_full.reshape(1, bp, seq + N_META, NH_D, DV_D),
        c_main[None],
        n_main_st[None, :, :NH_M, :],
        m_main[None, :, :NH_M, 0],
        xm_main3[None, :, seq - (CONV_W - 1):, :],
        k_smp.reshape(1, bs, ts, NH_D, DV_D),
        v_smp.reshape(1, bs, ts, NH_D, DV_D),
        c_smp[None],
        n_smp[None, :, :NH_M, :],
        m_smp[None, :, :NH_M, 0],
        xm_smp.reshape(bs, ts, D_MODEL)[None, :, ts - (CONV_W - 1):, :],
    )
```

```python
import functools
import math

import jax
import jax.numpy as jnp
from jax import lax
from jax.experimental import pallas as pl
from jax.experimental.pallas import tpu as pltpu

F32 = jnp.float32
BF16 = jnp.bfloat16

D_MODEL = 1024
N_META = 16
EPS = 1e-6
D_FF = 2816
NH_M = 4
DH_M = 256
CONV_W = 4
NH_D = 8
DH_D = 64
DV_D = 128
ROT_DIM = 16
ROPE_THETA = 500000.0
LAMBDA_INIT = 0.8 - 0.6 * math.exp(-0.3 * 0)

LANES = 128
SUBLANES = 8
NEG_BIG = -1e30
VMEM_LIMIT = 56 * 1024 * 1024


def _cparams(sem, flags=None):
    return pltpu.CompilerParams(dimension_semantics=sem, vmem_limit_bytes=VMEM_LIMIT, flags=flags)


def _rms(x, g):
    return x * lax.rsqrt(jnp.mean(x * x, axis=-1, keepdims=True) + EPS) * g


def _sigmoid(x):
    return 1.0 / (1.0 + jnp.exp(-x))


def _dot(a, b):
    return jnp.dot(a, b, preferred_element_type=F32)


def _dot_nt(a, b):
    return lax.dot_general(a, b, (((1,), (1,)), ((), ())), preferred_element_type=F32)


def _row_tile(n, want):
    if n <= 2 * want:
        return n
    t = want
    while n % t:
        t //= 2
    assert t == n or t % SUBLANES == 0
    return t


MLSTM_CHUNK = 256
MLSTM_GROUP = 8


def _group(n):
    return max(g for g in range(1, MLSTM_GROUP + 1) if n % g == 0)


def _full(shape):
    zeros = (0,) * len(shape)
    return pl.BlockSpec(shape, lambda *_: zeros)


def _resident(shape):
    zeros = (0,) * len(shape)
    return pl.BlockSpec(shape, lambda *_: zeros, pipeline_mode=pl.Buffered(1))


def _ffn_body(x_ref, g_ref, wg_ref, wu_ref, wd_ref, fg_ref, o_ref, *, final_norm):
    x = x_ref[...]
    xb = _rms(x, g_ref[...]).astype(BF16)
    g = _dot(xb, wg_ref[...])
    u = _dot(xb, wu_ref[...])
    a = (g * _sigmoid(g) * u).astype(BF16)
    y = x + 0.5 * _dot(a, wd_ref[...])
    if final_norm:
        y = _rms(y, fg_ref[...])
    o_ref[...] = y


def _ffn(x, gain, wg, wu, wd, final_gain, *, final_norm, tm=1024):
    n = x.shape[0]
    tm = _row_tile(n, tm)
    row = pl.BlockSpec((tm, D_MODEL), lambda i: (i, 0))
    return pl.pallas_call(
        functools.partial(_ffn_body, final_norm=final_norm),
        grid=(n // tm,),
        in_specs=[row, _full((1, D_MODEL)), _resident(wg.shape), _resident(wu.shape), _resident(wd.shape),
                  _full((1, D_MODEL))],
        out_specs=row,
        out_shape=jax.ShapeDtypeStruct((n, D_MODEL), F32),
        compiler_params=_cparams(("parallel",)),
        name="ffn_final" if final_norm else "ffn",
    )(x, gain, wg, wu, wd, final_gain)


def _rope_slab(xs, c, s_up, s_dn):
    return xs * c + pltpu.roll(xs, LANES - ROT_DIM // 2, axis=1) * s_up + pltpu.roll(xs, ROT_DIM // 2, axis=1) * s_dn


def _inproj_body(x_ref, g_ref, wxm_ref, wom_ref, wq_ref, wk_ref, wv_ref, wga_ref, wgb_ref, wgate_ref,
                 gbias_ref, cq_ref, cu_ref, cd_ref,
                 xm_ref, om_ref, q_ref, k_ref, v_ref, ga_ref, gb_ref, gcol_ref, grow_ref):
    hb = _rms(x_ref[...], g_ref[...]).astype(BF16)
    xm_ref[...] = _dot(hb, wxm_ref[...])
    om_ref[...] = _sigmoid(_dot(hb, wom_ref[...]))
    v_ref[...] = _dot(hb, wv_ref[...])
    ga_ref[...] = _dot(hb, wga_ref[...])
    gb_ref[...] = _dot(hb, wgb_ref[...])
    c, s_up, s_dn = cq_ref[...], cu_ref[...], cd_ref[...]
    q = _dot(hb, wq_ref[...])
    k = _dot(hb, wk_ref[...])
    qscale = DH_D ** -0.5 * math.log2(math.e)
    for h in range(NH_D):
        sl = slice(h * LANES, (h + 1) * LANES)
        q_ref[:, sl] = (_rope_slab(q[:, sl], c, s_up, s_dn) * qscale).astype(BF16)
        k_ref[:, sl] = _rope_slab(k[:, sl], c, s_up, s_dn)
    pre = _dot(hb, wgate_ref[...]) + gbias_ref[...]
    lane = lax.broadcasted_iota(jnp.int32, pre.shape, 1)
    logsig = jnp.minimum(pre, 0.0) - jnp.log(1.0 + jnp.exp(-jnp.abs(pre)))
    glog = jnp.where(lane < NH_M, pre, logsig)
    gcol_ref[...] = glog
    grow_ref[...] = glog.T[:SUBLANES, :]


def _inproj(x, gain, w, gbias, rope_tabs, *, n_rope_blocks, tm):
    n = x.shape[0]
    assert n % tm == 0
    row = pl.BlockSpec((tm, D_MODEL), lambda i: (i, 0))
    tab = pl.BlockSpec((tm, LANES), lambda i: (i % n_rope_blocks, 0))
    wspec = _resident((D_MODEL, D_MODEL))
    out_shapes = (
        jax.ShapeDtypeStruct((n, D_MODEL), F32),
        jax.ShapeDtypeStruct((n, D_MODEL), F32),
        jax.ShapeDtypeStruct((n, D_MODEL), BF16),
        jax.ShapeDtypeStruct((n, D_MODEL), F32),
        jax.ShapeDtypeStruct((n, D_MODEL), F32),
        jax.ShapeDtypeStruct((n, D_MODEL), F32),
        jax.ShapeDtypeStruct((n, D_MODEL), F32),
        jax.ShapeDtypeStruct((n, LANES), F32),
        jax.ShapeDtypeStruct((SUBLANES, n), F32),
    )
    return pl.pallas_call(
        _inproj_body,
        grid=(n // tm,),
        in_specs=[row, _full((1, D_MODEL))] + [wspec] * 7 + [_full((D_MODEL, LANES)), _full((1, LANES)),
                                                              tab, tab, tab],
        out_specs=(row, row, row, row, row, row, row,
                   pl.BlockSpec((tm, LANES), lambda i: (i, 0)),
                   pl.BlockSpec((SUBLANES, tm), lambda i: (0, i))),
        out_shape=out_shapes,
        compiler_params=_cparams(("parallel",)),
        name="inproj",
    )(x, gain, w["xm"], w["om"], w["q"], w["k"], w["v"], w["ga"], w["gb"], w["gate"], gbias, *rope_tabs)


def _mlstm_body(xm_ref, xprev_ref, buf0_ref, om_ref, gcol_ref, grow_ref, cw_ref, cb_ref,
                wq_ref, wk_ref, wv_ref, wvt_ref, norm_ref, c0_ref, n0_ref, m0_ref,
                hm_ref, c_ref, n_ref, m_ref, ext_ref, *, chunk, shared_init):
    ci = pl.program_id(1)
    L = chunk
    nb = xm_ref.shape[0]

    @pl.when(ci == 0)
    def _():
        for bi in range(nb):
            src = 0 if shared_init else bi
            c_ref[bi] = c0_ref[src]
            n_ref[bi] = n0_ref[src]
            m_ref[bi] = m0_ref[src]
            ext_ref[bi, 0:SUBLANES, :] = buf0_ref[src]

    @pl.when(ci > 0)
    def _():
        ext_ref[:, 0:SUBLANES, :] = xprev_ref[...]

    x = xm_ref[...]
    ext_ref[:, SUBLANES:SUBLANES + L, :] = x
    e = ext_ref[...]
    cv = e * cw_ref[0:1, :]
    for j in range(1, CONV_W):
        cv = pltpu.roll(cv, 1, axis=1) + e * cw_ref[j:j + 1, :]
    cv = cv[:, SUBLANES:, :] + cb_ref[...]
    cact = cv * _sigmoid(cv)

    ti = lax.broadcasted_iota(jnp.int32, (L, L), 0)
    si = lax.broadcasted_iota(jnp.int32, (L, L), 1)
    tril = si <= ti

    for h in range(NH_M):
        hs = slice(h * DH_M, (h + 1) * DH_M)
        chb = cact[:, :, hs].reshape(nb * L, DH_M).astype(BF16)
        xhb = x[:, :, hs].reshape(nb * L, DH_M).astype(BF16)
        q_all = _dot(chb, wq_ref[h])
        k_all = _dot(chb, wk_ref[h])
        v_all = _dot(xhb, wv_ref[h])
        for bi in range(nb):
            rows = slice(bi * L, (bi + 1) * L)
            q, k, v = q_all[rows], k_all[rows], v_all[rows]
            vt = _dot_nt(wvt_ref[h], xhb[rows])
            qb, kb, vb = q.astype(BF16), k.astype(BF16), v.astype(BF16)

            gcol = gcol_ref[bi]
            grow = grow_ref[bi, 0]
            logi_c, logf_c = gcol[:, h:h + 1], gcol[:, NH_M + h:NH_M + h + 1]
            logi_r, logf_r = grow[h:h + 1, :], grow[NH_M + h:NH_M + h + 1, :]
            b_c = jnp.sum(jnp.where(tril, logf_r, 0.0), axis=1, keepdims=True)
            b_r = jnp.sum(jnp.where(ti <= si, logf_c, 0.0), axis=0, keepdims=True)
            m_prev = m_ref[bi, h:h + 1, 0:1]
            n_prev = n_ref[bi, h:h + 1, :]
            c_prev = c_ref[bi, h]

            d = jnp.where(tril, b_c - b_r + logi_r, -jnp.inf)
            inter = b_c + m_prev
            m_t = jnp.maximum(jnp.max(d, axis=1, keepdims=True), inter)
            w_intra = jnp.exp(d - m_t)
            w_inter = jnp.exp(inter - m_t)
            s = _dot_nt(qb, kb) * w_intra
            num = _dot(s.astype(BF16), vb) + w_inter * _dot_nt(qb, c_prev.astype(BF16))
            den = jnp.sum(s, axis=1, keepdims=True) + w_inter * jnp.sum(q * n_prev, axis=1, keepdims=True)
            hh = num / jnp.maximum(jnp.abs(den), jnp.exp(-m_t))

            m_new = m_t[L - 1:L, :]
            b_last = b_c[L - 1:L, :]
            w_state_r = jnp.exp(b_last - b_r + logi_r - m_new)
            w_state_c = jnp.exp(b_last - b_c + logi_c - m_new)
            decay = jnp.exp(b_last + m_prev - m_new)
            c_ref[bi, h] = decay * c_prev + _dot((vt * w_state_r).astype(BF16), kb)
            n_ref[bi, h:h + 1, :] = decay * n_prev + jnp.sum(k * w_state_c, axis=0, keepdims=True)
            m_ref[bi, h:h + 1, :] = jnp.broadcast_to(m_new, (1, LANES))

            hn = hh * lax.rsqrt(jnp.mean(hh * hh, axis=1, keepdims=True) + EPS) * norm_ref[:, hs]
            hm_ref[bi, :, hs] = (hn * om_ref[bi, :, hs]).astype(BF16)


def _mlstm(xm, om, gcol, grow, buf0, c0, n0, m0, w, *, chunk, nb):
    B, T, _ = xm.shape
    L = chunk
    nc = T // L
    shared = c0.shape[0] == 1 and B > 1
    nb0 = 1 if shared else nb
    st = (lambda b, c: (0, 0, 0)) if shared else (lambda b, c: (b, 0, 0))
    st4 = (lambda b, c: (0, 0, 0, 0)) if shared else (lambda b, c: (b, 0, 0, 0))
    seq = pl.BlockSpec((nb, L, D_MODEL), lambda b, c: (b, c, 0))
    wsq = _full((NH_M, DH_M, DH_M))
    return pl.pallas_call(
        functools.partial(_mlstm_body, chunk=L, shared_init=shared),
        grid=(B // nb, nc),
        in_specs=[
            seq,
            pl.BlockSpec((nb, SUBLANES, D_MODEL), lambda b, c: (b, jnp.maximum(c * (L // SUBLANES) - 1, 0), 0)),
            pl.BlockSpec((nb0, SUBLANES, D_MODEL), st),
            seq,
            pl.BlockSpec((nb, L, LANES), lambda b, c: (b, c, 0)),
            pl.BlockSpec((nb, 1, SUBLANES, L), lambda b, c: (b, c, 0, 0)),
            _full((CONV_W, D_MODEL)), _full((1, D_MODEL)),
            wsq, wsq, wsq, wsq, _full((1, D_MODEL)),
            pl.BlockSpec((nb0, NH_M, DH_M, DH_M), st4),
            pl.BlockSpec((nb0, SUBLANES, DH_M), st),
            pl.BlockSpec((nb0, SUBLANES, LANES), st),
        ],
        out_specs=(
            seq,
            pl.BlockSpec((nb, NH_M, DH_M, DH_M), lambda b, c: (b, 0, 0, 0)),
            pl.BlockSpec((nb, SUBLANES, DH_M), lambda b, c: (b, 0, 0)),
            pl.BlockSpec((nb, SUBLANES, LANES), lambda b, c: (b, 0, 0)),
        ),
        out_shape=(
            jax.ShapeDtypeStruct((B, T, D_MODEL), BF16),
            jax.ShapeDtypeStruct((B, NH_M, DH_M, DH_M), F32),
            jax.ShapeDtypeStruct((B, SUBLANES, DH_M), F32),
            jax.ShapeDtypeStruct((B, SUBLANES, LANES), F32),
        ),
        scratch_shapes=[pltpu.VMEM((nb, SUBLANES + L, D_MODEL), F32)],
        compiler_params=_cparams(("parallel", "arbitrary")),
        name="mlstm",
    )(xm, xm, buf0, om, gcol, grow, w["conv_w"], w["conv_b"], w["mq"], w["mk"], w["mv"], w["mvt"],
      w["mnorm"], c0, n0, m0)


def _lam(lq1_ref, lk1_ref, lq2_ref, lk2_ref):
    return (jnp.exp(jnp.sum(lq1_ref[...] * lk1_ref[...], axis=1, keepdims=True))
            - jnp.exp(jnp.sum(lq2_ref[...] * lk2_ref[...], axis=1, keepdims=True)) + LAMBDA_INIT)


def _online_softmax_step(m_ref, l_ref, acc_ref, s, vb):
    m_old = m_ref[...]
    m_new = jnp.maximum(m_old, jnp.max(s, axis=1, keepdims=True))
    pr = jnp.exp2(s - m_new)
    r = jnp.exp2(m_old - m_new)
    l_ref[...] = r * l_ref[...] + jnp.sum(pr, axis=1, keepdims=True)
    acc_ref[...] = r * acc_ref[...] + _dot(pr.astype(BF16), vb)
    m_ref[...] = m_new


ATTN_TQ = 1024
ONES_ROWS = 16


def _attn_prompt_body(q_ref, k_ref, v_ref, kp_ref, vp_ref, vpt_ref, lq1_ref, lk1_ref, lq2_ref, lk2_ref, subln_ref,
                      o_ref, kf_ref, vf_ref, kb_ref, vt_ref, m_ref, acc_ref, sa_ref, sb_ref, ma_ref, mb_ref, *, tq, tk,
                      has_prefix):
    qi = pl.program_id(2)
    nblk = vt_ref.shape[0]

    def with_ones(vt):
        keys = vt.shape[1]
        row = lax.broadcasted_iota(jnp.int32, (ONES_ROWS, keys), 0)
        return jnp.concatenate([vt, jnp.where(row == 0, 1.0, 0.0)], axis=0).astype(BF16)

    @pl.when(qi == 0)
    def _():
        npre = kp_ref.shape[0]
        kf_ref[0, 0:npre, :] = kp_ref[...]
        vf_ref[0, 0:npre, :] = vp_ref[...]
        kf_ref[0, npre:, :] = k_ref[0]
        vf_ref[0, npre:, :] = v_ref[0]
        kb_ref[...] = k_ref[0].astype(BF16)
        for c in range(nblk):
            vt_ref[c] = with_ones(v_ref[0, c * tk:(c + 1) * tk, :].T)

    q = q_ref[0].astype(F32)
    lane = lax.broadcasted_iota(jnp.int32, q.shape, 1)
    qq = jnp.concatenate([jnp.where(lane < DH_D, q, 0.0), jnp.where(lane < DH_D, 0.0, q)], axis=0)
    qqt = qq.T.astype(BF16)
    m_ref[...] = jnp.full(m_ref.shape, -jnp.inf, F32)
    acc_ref[...] = jnp.zeros(acc_ref.shape, F32)

    def apply(st, mx, vt):
        m_old = m_ref[...]
        m_new = jnp.maximum(m_old, mx)
        acc_ref[...] = jnp.exp2(m_old - m_new) * acc_ref[...] + _dot(vt, jnp.exp2(st - m_new).astype(BF16))
        m_ref[...] = m_new

    def step(buf, j):
        s_ref, mx_ref = buf
        apply(s_ref[...], mx_ref[...], vt_ref[j])

    rel = (lax.broadcasted_iota(jnp.int32, (tk, 2 * tq), 0)
           - lax.broadcasted_iota(jnp.int32, (tk, 2 * tq), 1) % tq)

    def put(buf, j, masked):
        s_ref, mx_ref = buf
        st = _dot(kb_ref[pl.ds(pl.multiple_of(j * tk, tk), tk), :], qqt)
        if masked:
            st = jnp.where(rel <= qi * tq - j * tk, st, -jnp.inf)
        s_ref[...] = st
        mx_ref[...] = jnp.max(st, axis=0, keepdims=True)

    if has_prefix:
        st = _dot(kp_ref[...].astype(BF16), qqt)
        apply(st, jnp.max(st, axis=0, keepdims=True), with_ones(vpt_ref[...]))

    assert tq == 2 * tk
    buf_a, buf_b = (sa_ref, ma_ref), (sb_ref, mb_ref)
    put(buf_a, 0, True)

    def pair(p, carry):
        j = 2 * p
        put(buf_b, j + 1, False)
        step(buf_a, j)
        put(buf_a, j + 2, False)
        step(buf_b, j + 1)
        return carry

    lax.fori_loop(0, qi - 1, pair, 0)

    @pl.when(qi >= 1)
    def _():
        j = 2 * (qi - 1)
        put(buf_b, j + 1, False)
        step(buf_a, j)
        put(buf_a, j + 2, True)
        step(buf_b, j + 1)

    half = tq // 2
    cols = (slice(half, tq), slice(tq + half, 2 * tq))
    last = 2 * qi + 1
    qh = jnp.concatenate([qqt[:, c] for c in cols], axis=1)
    st = _dot(kb_ref[pl.ds(pl.multiple_of(last * tk, tk), tk), :], qh)
    tri = (lax.broadcasted_iota(jnp.int32, (tk, tq), 0)
           <= lax.broadcasted_iota(jnp.int32, (tk, tq), 1) % half)
    st = jnp.where(tri, st, -jnp.inf)
    mx = jnp.max(st, axis=0, keepdims=True)
    step(buf_a, 2 * qi)
    m_old = jnp.concatenate([m_ref[:, c] for c in cols], axis=1)
    m_new = jnp.maximum(m_old, mx)
    r = jnp.exp2(m_old - m_new)
    pv = _dot(vt_ref[last], jnp.exp2(st - m_new).astype(BF16))
    for i, c in enumerate(cols):
        part = slice(i * half, (i + 1) * half)
        acc_ref[:, c] = r[:, part] * acc_ref[:, c] + pv[:, part]
        m_ref[:, c] = m_new[:, part]

    lam = _lam(lq1_ref, lk1_ref, lq2_ref, lk2_ref)
    acc = acc_ref[0:DV_D, :] / acc_ref[DV_D:DV_D + 1, :]
    ot = acc[:, 0:tq] - lam * acc[:, tq:2 * tq]
    ot = ot * lax.rsqrt(jnp.mean(ot * ot, axis=0, keepdims=True) + EPS) * subln_ref[...]
    o_ref[0] = (ot * (1.0 - LAMBDA_INIT)).T.astype(BF16)


def _attn_prompt(q, k, v, kpre, vpre, lams, subln_col, *, tq, has_prefix):
    vpre_t = vpre.T
    B, T, _ = q.shape
    P = kpre.shape[0]
    tk = tq // 2
    assert T % tq == 0 and tq % (2 * tk) == 0
    lspec = _full((1, DH_D))
    return pl.pallas_call(
        functools.partial(_attn_prompt_body, tq=tq, tk=tk, has_prefix=has_prefix),
        grid=(B, NH_D, T // tq),
        in_specs=[
            pl.BlockSpec((1, tq, LANES), lambda b, h, i: (b, i, h)),
            pl.BlockSpec((1, T, LANES), lambda b, h, i: (b, 0, h)),
            pl.BlockSpec((1, T, LANES), lambda b, h, i: (b, 0, h)),
            pl.BlockSpec((P, LANES), lambda b, h, i: (0, h)),
            pl.BlockSpec((P, LANES), lambda b, h, i: (0, h)),
            pl.BlockSpec((DV_D, P), lambda b, h, i: (h, 0)),
            lspec, lspec, lspec, lspec, _full((DV_D, 1)),
        ],
        out_specs=(pl.BlockSpec((1, tq, LANES), lambda b, h, i: (b, i, h)),
                   pl.BlockSpec((1, P + T, LANES), lambda b, h, i: (b, 0, h)),
                   pl.BlockSpec((1, P + T, LANES), lambda b, h, i: (b, 0, h))),
        out_shape=(jax.ShapeDtypeStruct((B, T, D_MODEL), BF16),
                   jax.ShapeDtypeStruct((B, P + T, D_MODEL), F32),
                   jax.ShapeDtypeStruct((B, P + T, D_MODEL), F32)),
        scratch_shapes=[pltpu.VMEM((T, LANES), BF16), pltpu.VMEM((T // tk, DV_D + ONES_ROWS, tk), BF16),
                        pltpu.VMEM((1, 2 * tq), F32), pltpu.VMEM((DV_D + ONES_ROWS, 2 * tq), F32),
                        pltpu.VMEM((tk, 2 * tq), F32), pltpu.VMEM((tk, 2 * tq), F32),
                        pltpu.VMEM((1, 2 * tq), F32), pltpu.VMEM((1, 2 * tq), F32)],
        compiler_params=_cparams(("parallel", "parallel", "arbitrary")),
        name="attn_prompt",
    )(q, k, v, kpre, vpre, vpre_t, *lams, subln_col)


QROWS = 8
QSUB = 4
QMAT_ROWS = 2 * NH_D * QSUB
PAGES_PER_STEP = 16


def _attn_sample_body(pt_ref, q_ref, *refs, n_new, n_pg):
    kc_refs, vc_refs = refs[:n_pg], refs[n_pg:2 * n_pg]
    (kn_ref, vn_ref, lq1_ref, lk1_ref, lq2_ref, lk2_ref, subln_ref,
     o_ref, qm_ref, bias_ref, m_ref, l_ref, acc_ref) = refs[2 * n_pg:]
    p = pl.program_id(1)
    R = QMAT_ROWS
    rows = kc_refs[0].shape[1]
    half = R // 2

    def head_of(r):
        return (r % half) // QSUB

    @pl.when(p == 0)
    def _():
        q = q_ref[0].astype(F32)
        sub = lax.broadcasted_iota(jnp.int32, (QROWS, LANES), 0)
        pieces = []
        for j in range(NH_D // 2):
            even = q[:, (2 * j) * LANES:(2 * j + 1) * LANES]
            odd = q[:, (2 * j + 1) * LANES:(2 * j + 2) * LANES]
            pieces.append(jnp.where(sub < QSUB, even, pltpu.roll(odd, QSUB, axis=0)))
        qh = jnp.concatenate(pieces, axis=0)
        lane = lax.broadcasted_iota(jnp.int32, qh.shape, 1)
        qm_ref[...] = jnp.concatenate([jnp.where(lane < DH_D, qh, 0.0), jnp.where(lane < DH_D, 0.0, qh)],
                                      axis=0).astype(BF16)
        row = lax.broadcasted_iota(jnp.int32, (R, rows), 0)
        col = lax.broadcasted_iota(jnp.int32, (R, rows), 1)
        bias_ref[...] = jnp.where(col % NH_D == head_of(row), 0.0, -jnp.inf)
        m_ref[...] = jnp.full(m_ref.shape, -jnp.inf, F32)
        l_ref[...] = jnp.zeros(l_ref.shape, F32)
        acc_ref[...] = jnp.zeros(acc_ref.shape, F32)

    qm = qm_ref[...]
    bias = bias_ref[...]
    s = [_dot_nt(qm, kc[0].astype(BF16)) + bias for kc in kc_refs]
    m_old = m_ref[...]
    m_new = m_old
    for si in s:
        m_new = jnp.maximum(m_new, jnp.max(si, axis=1, keepdims=True))
    r = jnp.exp2(m_old - m_new)
    l_new = r * l_ref[...]
    acc = r * acc_ref[...]
    for si, vc in zip(s, vc_refs):
        pr = jnp.exp2(si - m_new)
        l_new = l_new + jnp.sum(pr, axis=1, keepdims=True)
        acc = acc + _dot(pr.astype(BF16), vc[0].astype(BF16))
    m_ref[...] = m_new
    l_ref[...] = l_new
    acc_ref[...] = acc

    @pl.when(p == pl.num_programs(1) - 1)
    def _():
        zpad = jnp.zeros((LANES - kn_ref.shape[1], LANES), F32)
        kn = jnp.concatenate([kn_ref[0], zpad], axis=0).astype(BF16)
        vn = jnp.concatenate([vn_ref[0], zpad], axis=0).astype(BF16)
        sn = _dot_nt(qm, kn)
        row = lax.broadcasted_iota(jnp.int32, sn.shape, 0)
        col = lax.broadcasted_iota(jnp.int32, sn.shape, 1)
        tok = col // NH_D
        ok = (col % NH_D == head_of(row)) & (tok <= row % QSUB) & (tok < n_new)
        _online_softmax_step(m_ref, l_ref, acc_ref, jnp.where(ok, sn, -jnp.inf), vn)
        lam = _lam(lq1_ref, lk1_ref, lq2_ref, lk2_ref)
        o = acc_ref[0:half, :] / l_ref[0:half, :] - lam * (acc_ref[half:R, :] / l_ref[half:R, :])
        o_ref[0] = (_rms(o, subln_ref[...]) * (1.0 - LAMBDA_INIT)).astype(BF16)


def _attn_sample(page_table, q, cache_k, cache_v, knew, vnew, lams, subln, *, n_new):
    B, n_pages = page_table.shape
    rows = cache_k.shape[1]
    R = QMAT_ROWS
    n_pg = math.gcd(n_pages, PAGES_PER_STEP)
    lspec = pl.BlockSpec((1, DH_D), lambda b, p, pt: (0, 0))
    pgs = [pl.BlockSpec((1, rows, LANES), lambda b, p, pt, i=i: (pt[b, p * n_pg + i], 0, 0)) for i in range(n_pg)]
    new = pl.BlockSpec((1, n_new * NH_D, LANES), lambda b, p, pt: (b, 0, 0))
    return pl.pallas_call(
        functools.partial(_attn_sample_body, n_new=n_new, n_pg=n_pg),
        grid_spec=pltpu.PrefetchScalarGridSpec(
            num_scalar_prefetch=1,
            grid=(B, n_pages // n_pg),
            in_specs=[pl.BlockSpec((1, QROWS, D_MODEL), lambda b, p, pt: (b, 0, 0))] + pgs + pgs
                     + [new, new, lspec, lspec, lspec, lspec, pl.BlockSpec((1, DV_D), lambda b, p, pt: (0, 0))],
            out_specs=pl.BlockSpec((1, R // 2, DV_D), lambda b, p, pt: (b, 0, 0)),
            scratch_shapes=[pltpu.VMEM((R, LANES), BF16), pltpu.VMEM((R, rows), F32), pltpu.VMEM((R, 1), F32),
                            pltpu.VMEM((R, 1), F32), pltpu.VMEM((R, DV_D), F32)],
        ),
        out_shape=jax.ShapeDtypeStruct((B, R // 2, DV_D), BF16),
        compiler_params=_cparams(("parallel", "arbitrary")),
        name="attn_sample",
    )(page_table, q, *([cache_k] * n_pg), *([cache_v] * n_pg), knew, vnew, *lams, subln)


def _merge_body(hm_ref, o_ref, ga_ref, gb_ref, x_ref, wa_ref, wb_ref, wo_ref, y_ref):
    ya = _dot(hm_ref[...], wa_ref[...])
    yb = _dot(o_ref[...], wb_ref[...])
    merged = _sigmoid(ga_ref[...]) * ya + _sigmoid(gb_ref[...]) * yb
    y_ref[...] = x_ref[...] + _dot(merged.astype(BF16), wo_ref[...])


def _merge(hm, o, ga, gb, x, wa, wb, wo, *, tm=1024):
    n = hm.shape[0]
    tm = _row_tile(n, tm)
    row = pl.BlockSpec((tm, D_MODEL), lambda i: (i, 0))
    wspec = _resident((D_MODEL, D_MODEL))
    return pl.pallas_call(
        _merge_body,
        grid=(n // tm,),
        in_specs=[row, row, row, row, row, wspec, wspec, wspec],
        out_specs=row,
        out_shape=jax.ShapeDtypeStruct((n, D_MODEL), F32),
        compiler_params=_cparams(("parallel",)),
        name="merge",
    )(hm, o, ga, gb, x, wa, wb, wo)


def _rope_tables(pos):
    half = ROT_DIM // 2
    inv_freq = ROPE_THETA ** (-jnp.arange(half, dtype=F32) / half)
    d = jnp.arange(LANES, dtype=jnp.int32) % DH_D
    freq = jnp.where(d < ROT_DIM, inv_freq[d % half], 0.0)
    ang = pos.astype(F32)[:, None] * freq[None, :]
    c, s = jnp.cos(ang), jnp.sin(ang)
    s_up = jnp.where(d < half, -s, 0.0)
    s_dn = jnp.where(d >= half, s, 0.0)
    return c, s_up, s_dn


def _pad_rows(a, rows, axis, front=False):
    pad = [(0, 0)] * a.ndim
    pad[axis] = (rows - a.shape[axis], 0) if front else (0, rows - a.shape[axis])
    return jnp.pad(a, pad)


def kernel(x_prompt, x_sample, cache_k, cache_v, page_table, state_C, state_n, state_m, state_conv, meta_tokens, ffn1_norm, ffn1_w_gate, ffn1_w_up, ffn1_w_down, mix_norm, w_in, conv_w, conv_b, w_mq, w_mk, w_mv, b_igate, b_fgate, mlstm_norm, lambda_q1, lambda_k1, lambda_q2, lambda_k2, subln, w_proj_a, w_proj_b, w_out, ffn2_norm, ffn2_w_gate, ffn2_w_up, ffn2_w_down, final_norm):
    bp, seq, _ = x_prompt.shape
    bs, ts, _ = x_sample.shape
    n_pages = page_table.shape[1]
    page = cache_k.shape[2]
    past_len = n_pages * page
    assert w_in.shape[0] == 1 and ts <= QSUB

    win = w_in[0]
    offs = [0]
    for width in (D_MODEL, D_MODEL, NH_M, NH_M, D_MODEL, D_MODEL, D_MODEL, D_MODEL, D_MODEL):
        offs.append(offs[-1] + width)
    col = lambda i: win[:, offs[i]:offs[i + 1]]
    wgate = jnp.pad(jnp.concatenate([col(2), col(3)], axis=1), ((0, 0), (0, LANES - 2 * NH_M)))
    wproj = {"xm": col(0).astype(BF16), "om": col(1).astype(BF16), "q": col(4).astype(BF16),
             "k": col(5).astype(BF16), "v": col(6).astype(BF16), "ga": col(7).astype(BF16),
             "gb": col(8).astype(BF16), "gate": wgate.astype(BF16)}
    gbias = jnp.pad(jnp.concatenate([b_igate[0], b_fgate[0]])[None, :], ((0, 0), (0, LANES - 2 * NH_M)))
    wm = {"conv_w": conv_w[0], "conv_b": conv_b[0][None, :],
          "mq": w_mq[0].astype(BF16), "mk": (w_mk[0] * DH_M ** -0.5).astype(BF16), "mv": w_mv[0].astype(BF16),
          "mvt": jnp.swapaxes(w_mv[0], 1, 2).astype(BF16), "mnorm": mlstm_norm[0].reshape(1, D_MODEL)}
    lams = (lambda_q1, lambda_k1, lambda_q2, lambda_k2)
    row = lambda a: a.reshape(1, -1)
    f1 = (row(ffn1_norm[0]), ffn1_w_gate[0].astype(BF16), ffn1_w_up[0].astype(BF16), ffn1_w_down[0].astype(BF16))
    f2 = (row(ffn2_norm[0]), ffn2_w_gate[0].astype(BF16), ffn2_w_up[0].astype(BF16), ffn2_w_down[0].astype(BF16))
    fin = row(final_norm)
    wa, wb, wo = w_proj_a[0].astype(BF16), w_proj_b[0].astype(BF16), w_out[0].astype(BF16)

    n_main = bp * seq
    n_samp = bs * ts
    x_main = x_prompt.reshape(n_main, D_MODEL)
    x_small = jnp.concatenate([x_sample.reshape(n_samp, D_MODEL), meta_tokens], axis=0)

    tm_in = min(seq, 512)
    tabs_main = _rope_tables(N_META + jnp.arange(seq, dtype=jnp.int32))
    pos_small = jnp.concatenate([jnp.tile(past_len + jnp.arange(ts, dtype=jnp.int32), bs),
                                 jnp.arange(N_META, dtype=jnp.int32)])
    tabs_small = _rope_tables(pos_small)

    x1_main = _ffn(x_main, *f1, fin, final_norm=False)
    x1_small = _ffn(x_small, *f1, fin, final_norm=False)
    pm = _inproj(x1_main, row(mix_norm[0]), wproj, gbias, tabs_main, n_rope_blocks=seq // tm_in, tm=tm_in)
    ps = _inproj(x1_small, row(mix_norm[0]), wproj, gbias, tabs_small, n_rope_blocks=1, tm=x_small.shape[0])
    xm_m, om_m, q_m, k_m, v_m, ga_m, gb_m, gcol_m, grow_m = pm
    xm_s, om_s, q_s, k_s, v_s, ga_s, gb_s, gcol_s, grow_s = ps

    def split(a):
        return a[:n_samp], a[n_samp:]

    xm_smp, xm_meta = split(xm_s)
    om_smp, om_meta = split(om_s)
    gcol_smp, gcol_meta = split(gcol_s)
    grow_smp, grow_meta = grow_s[:, :n_samp], grow_s[:, n_samp:]

    zc = jnp.zeros((1, NH_M, DH_M, DH_M), F32)
    zn = jnp.zeros((1, SUBLANES, DH_M), F32)
    zm = jnp.zeros((1, SUBLANES, LANES), F32)
    zbuf = jnp.zeros((1, SUBLANES, D_MODEL), F32)
    hm_meta, c_meta, n_meta, m_meta = _mlstm(
        xm_meta[None], om_meta[None], gcol_meta[None], grow_meta.reshape(1, 1, SUBLANES, N_META),
        zbuf, zc, zn, zm, wm, chunk=N_META, nb=1)

    chunk = _row_tile(seq, MLSTM_CHUNK)
    grow_main = grow_m.reshape(SUBLANES, bp, seq // chunk, chunk).transpose(1, 2, 0, 3)
    buf_main = _pad_rows(xm_meta[N_META - (CONV_W - 1):], SUBLANES, 0, front=True)[None]
    hm_main, c_main, n_main_st, m_main = _mlstm(
        xm_m.reshape(bp, seq, D_MODEL), om_m.reshape(bp, seq, D_MODEL), gcol_m.reshape(bp, seq, LANES),
        grow_main, buf_main, c_meta, n_meta, m_meta, wm, chunk=chunk, nb=_group(bp))

    pad_t = lambda a: _pad_rows(a.reshape(bs, ts, a.shape[-1]), QROWS, 1)
    gcol_pad = jnp.concatenate(
        [gcol_smp.reshape(bs, ts, LANES),
         jnp.broadcast_to(jnp.where(jnp.arange(LANES) < NH_M, NEG_BIG, 0.0).astype(F32), (bs, QROWS - ts, LANES))],
        axis=1)
    grow_pad = jnp.concatenate(
        [grow_smp.reshape(SUBLANES, bs, ts),
         jnp.broadcast_to(jnp.where(jnp.arange(SUBLANES) < NH_M, NEG_BIG, 0.0).astype(F32)[:, None, None],
                          (SUBLANES, bs, QROWS - ts))], axis=2).transpose(1, 0, 2)[:, None]
    hm_smp, c_smp, n_smp, m_smp = _mlstm(
        pad_t(xm_smp), pad_t(om_smp), gcol_pad, grow_pad,
        _pad_rows(state_conv[0], SUBLANES, 1, front=True), state_C[0],
        _pad_rows(state_n[0], SUBLANES, 1), jnp.broadcast_to(_pad_rows(state_m[0], SUBLANES, 1)[:, :, None],
                                                             (bs, SUBLANES, LANES)),
        wm, chunk=QROWS, nb=_group(bs))

    q_smp, q_meta = split(q_s)
    k_smp, k_meta = split(k_s)
    v_smp, v_meta = split(v_s)
    sub = row(subln[0])
    sub_col = subln[0].reshape(DV_D, 1)
    o_meta = _attn_prompt(q_meta[None], k_meta[None], v_meta[None], k_meta, v_meta, lams, sub_col,
                          tq=N_META, has_prefix=False)[0]
    tq = min(seq, ATTN_TQ)
    o_main, k_full, v_full = _attn_prompt(q_m.reshape(bp, seq, D_MODEL), k_m.reshape(bp, seq, D_MODEL),
                                          v_m.reshape(bp, seq, D_MODEL), k_meta, v_meta, lams, sub_col,
                                          tq=tq, has_prefix=True)
    del o_meta
    k_new = k_smp.reshape(1, bs, ts, NH_D, DV_D)
    v_new = v_smp.reshape(1, bs, ts, NH_D, DV_D)
    o_smp = _attn_sample(page_table, pad_t(q_smp), cache_k.reshape(-1, page * NH_D, DV_D),
                         cache_v.reshape(-1, page * NH_D, DV_D), k_new.reshape(bs, ts * NH_D, DV_D),
                         v_new.reshape(bs, ts * NH_D, DV_D), lams, sub, n_new=ts)
    o_smp = o_smp.reshape(bs, NH_D, QSUB, DV_D)[:, :, :ts].transpose(0, 2, 1, 3).reshape(n_samp, D_MODEL)

    x2_main = _merge(hm_main.reshape(n_main, D_MODEL), o_main.reshape(n_main, D_MODEL), ga_m, gb_m, x1_main,
                     wa, wb, wo)
    y_main = _ffn(x2_main, *f2, fin, final_norm=True)
    x2_smp = _merge(hm_smp[:, :ts].reshape(n_samp, D_MODEL), o_smp,
                    ga_s, gb_s, x1_small, wa, wb, wo)
    y_smp = _ffn(x2_smp, *f2, fin, final_norm=True)
    del hm_meta

    xm_main3 = xm_m.reshape(bp, seq, D_MODEL)
    return (
        y_main.reshape(bp, seq, D_MODEL),
        y_smp.reshape(bs, ts, D_MODEL),
        k_full.reshape(1, bp, seq + N_META, NH_D, DV_D),
        v_full.reshape(1, bp, seq + N_META, NH_D, DV_D),
        c_main[None],
        n_main_st[None, :, :NH_M, :],
        m_main[None, :, :NH_M, 0],
        xm_main3[None, :, seq - (CONV_W - 1):, :],
        k_new,
        v_new,
        c_smp[None],
        n_smp[None, :, :NH_M, :],
        m_smp[None, :, :NH_M, 0],
        xm_smp.reshape(bs, ts, D_MODEL)[None, :, ts - (CONV_W - 1):, :],
    )
```

```python
import functools
import math

import jax
import jax.numpy as jnp
from jax import lax
from jax.experimental import pallas as pl
from jax.experimental.pallas import tpu as pltpu

F32 = jnp.float32
BF16 = jnp.bfloat16

D_MODEL = 1024
N_META = 16
EPS = 1e-6
D_FF = 2816
NH_M = 4
DH_M = 256
CONV_W = 4
NH_D = 8
DH_D = 64
DV_D = 128
ROT_DIM = 16
ROPE_THETA = 500000.0
LAMBDA_INIT = 0.8 - 0.6 * math.exp(-0.3 * 0)

LANES = 128
SUBLANES = 8
NEG_BIG = -1e30
V7X_VMEM_BYTES = 64 * 1024 * 1024
VMEM_LIMIT = V7X_VMEM_BYTES - 8 * 1024 * 1024


def _cparams(sem):
    return pltpu.CompilerParams(dimension_semantics=sem, vmem_limit_bytes=VMEM_LIMIT)


def _rms(x, g):
    return x * lax.rsqrt(jnp.mean(x * x, axis=-1, keepdims=True) + EPS) * g


def _sigmoid(x):
    return 1.0 / (1.0 + jnp.exp(-x))


def _dot(a, b):
    return jnp.dot(a, b, preferred_element_type=F32)


def _dot_nt(a, b):
    return lax.dot_general(a, b, (((1,), (1,)), ((), ())), preferred_element_type=F32)


def _row_tile(n, want):
    if n <= 2 * want:
        return n
    t = want
    while n % t:
        t //= 2
    assert t == n or t % SUBLANES == 0
    return t


MLSTM_CHUNK = 256
MLSTM_GROUP = 8


def _group(n):
    return max(g for g in range(1, MLSTM_GROUP + 1) if n % g == 0)


def _full(shape):
    zeros = (0,) * len(shape)
    return pl.BlockSpec(shape, lambda *_: zeros)


def _resident(shape):
    zeros = (0,) * len(shape)
    return pl.BlockSpec(shape, lambda *_: zeros, pipeline_mode=pl.Buffered(1))


def _ffn_body(x_ref, g_ref, wg_ref, wu_ref, wd_ref, fg_ref, o_ref, *, final_norm):
    x = x_ref[...]
    xb = _rms(x, g_ref[...]).astype(BF16)
    g = _dot(xb, wg_ref[...])
    u = _dot(xb, wu_ref[...])
    a = (g * _sigmoid(g) * u).astype(BF16)
    y = x + 0.5 * _dot(a, wd_ref[...])
    if final_norm:
        y = _rms(y, fg_ref[...])
    o_ref[...] = y


def _ffn(x, gain, wg, wu, wd, final_gain, *, final_norm, tm=1024):
    n = x.shape[0]
    tm = _row_tile(n, tm)
    row = pl.BlockSpec((tm, D_MODEL), lambda i: (i, 0))
    return pl.pallas_call(
        functools.partial(_ffn_body, final_norm=final_norm),
        grid=(n // tm,),
        in_specs=[row, _full((1, D_MODEL)), _resident(wg.shape), _resident(wu.shape), _resident(wd.shape),
                  _full((1, D_MODEL))],
        out_specs=row,
        out_shape=jax.ShapeDtypeStruct((n, D_MODEL), F32),
        compiler_params=_cparams(("parallel",)),
        name="ffn_final" if final_norm else "ffn",
    )(x, gain, wg, wu, wd, final_gain)


def _rope_slab(xs, c, s_up, s_dn):
    return xs * c + pltpu.roll(xs, LANES - ROT_DIM // 2, axis=1) * s_up + pltpu.roll(xs, ROT_DIM // 2, axis=1) * s_dn


def _inproj_body(x_ref, g_ref, wxm_ref, wom_ref, wq_ref, wk_ref, wv_ref, wga_ref, wgb_ref, wgate_ref,
                 gbias_ref, cq_ref, cu_ref, cd_ref,
                 xm_ref, om_ref, q_ref, k_ref, v_ref, ga_ref, gb_ref, gcol_ref, grow_ref):
    hb = _rms(x_ref[...], g_ref[...]).astype(BF16)
    xm_ref[...] = _dot(hb, wxm_ref[...])
    om_ref[...] = _sigmoid(_dot(hb, wom_ref[...]))
    v_ref[...] = _dot(hb, wv_ref[...])
    ga_ref[...] = _dot(hb, wga_ref[...])
    gb_ref[...] = _dot(hb, wgb_ref[...])
    c, s_up, s_dn = cq_ref[...], cu_ref[...], cd_ref[...]
    q = _dot(hb, wq_ref[...])
    k = _dot(hb, wk_ref[...])
    qscale = DH_D ** -0.5 * math.log2(math.e)
    for h in range(NH_D):
        sl = slice(h * LANES, (h + 1) * LANES)
        q_ref[:, sl] = (_rope_slab(q[:, sl], c, s_up, s_dn) * qscale).astype(BF16)
        k_ref[:, sl] = _rope_slab(k[:, sl], c, s_up, s_dn)
    pre = _dot(hb, wgate_ref[...]) + gbias_ref[...]
    lane = lax.broadcasted_iota(jnp.int32, pre.shape, 1)
    logsig = jnp.minimum(pre, 0.0) - jnp.log(1.0 + jnp.exp(-jnp.abs(pre)))
    glog = jnp.where(lane < NH_M, pre, logsig)
    gcol_ref[...] = glog
    grow_ref[...] = glog.T[:SUBLANES, :]


def _inproj(x, gain, w, gbias, rope_tabs, *, n_rope_blocks, tm):
    n = x.shape[0]
    assert n % tm == 0
    row = pl.BlockSpec((tm, D_MODEL), lambda i: (i, 0))
    tab = pl.BlockSpec((tm, LANES), lambda i: (i % n_rope_blocks, 0))
    wspec = _resident((D_MODEL, D_MODEL))
    out_shapes = (
        jax.ShapeDtypeStruct((n, D_MODEL), F32),
        jax.ShapeDtypeStruct((n, D_MODEL), F32),
        jax.ShapeDtypeStruct((n, D_MODEL), BF16),
        jax.ShapeDtypeStruct((n, D_MODEL), F32),
        jax.ShapeDtypeStruct((n, D_MODEL), F32),
        jax.ShapeDtypeStruct((n, D_MODEL), F32),
        jax.ShapeDtypeStruct((n, D_MODEL), F32),
        jax.ShapeDtypeStruct((n, LANES), F32),
        jax.ShapeDtypeStruct((SUBLANES, n), F32),
    )
    return pl.pallas_call(
        _inproj_body,
        grid=(n // tm,),
        in_specs=[row, _full((1, D_MODEL))] + [wspec] * 7 + [_full((D_MODEL, LANES)), _full((1, LANES)),
                                                              tab, tab, tab],
        out_specs=(row, row, row, row, row, row, row,
                   pl.BlockSpec((tm, LANES), lambda i: (i, 0)),
                   pl.BlockSpec((SUBLANES, tm), lambda i: (0, i))),
        out_shape=out_shapes,
        compiler_params=_cparams(("parallel",)),
        name="inproj",
    )(x, gain, w["xm"], w["om"], w["q"], w["k"], w["v"], w["ga"], w["gb"], w["gate"], gbias, *rope_tabs)


def _mlstm_body(xm_ref, xprev_ref, buf0_ref, om_ref, gcol_ref, grow_ref, cw_ref, cb_ref,
                wq_ref, wk_ref, wv_ref, wvt_ref, norm_ref, c0_ref, n0_ref, m0_ref,
                hm_ref, c_ref, n_ref, m_ref, ext_ref, *, chunk, shared_init):
    ci = pl.program_id(1)
    L = chunk
    nb = xm_ref.shape[0]

    @pl.when(ci == 0)
    def _():
        for bi in range(nb):
            src = 0 if shared_init else bi
            c_ref[bi] = c0_ref[src]
            n_ref[bi] = n0_ref[src]
            m_ref[bi] = m0_ref[src]
            ext_ref[bi, 0:SUBLANES, :] = buf0_ref[src]

    @pl.when(ci > 0)
    def _():
        ext_ref[:, 0:SUBLANES, :] = xprev_ref[...]

    x = xm_ref[...]
    ext_ref[:, SUBLANES:SUBLANES + L, :] = x
    e = ext_ref[...]
    cv = e * cw_ref[0:1, :]
    for j in range(1, CONV_W):
        cv = pltpu.roll(cv, 1, axis=1) + e * cw_ref[j:j + 1, :]
    cv = cv[:, SUBLANES:, :] + cb_ref[...]
    cact = cv * _sigmoid(cv)

    ti = lax.broadcasted_iota(jnp.int32, (L, L), 0)
    si = lax.broadcasted_iota(jnp.int32, (L, L), 1)
    tril = si <= ti

    for h in range(NH_M):
        hs = slice(h * DH_M, (h + 1) * DH_M)
        chb = cact[:, :, hs].reshape(nb * L, DH_M).astype(BF16)
        xhb = x[:, :, hs].reshape(nb * L, DH_M).astype(BF16)
        q_all = _dot(chb, wq_ref[h])
        k_all = _dot(chb, wk_ref[h])
        v_all = _dot(xhb, wv_ref[h])
        for bi in range(nb):
            rows = slice(bi * L, (bi + 1) * L)
            q, k, v = q_all[rows], k_all[rows], v_all[rows]
            vt = _dot_nt(wvt_ref[h], xhb[rows])
            qb, kb, vb = q.astype(BF16), k.astype(BF16), v.astype(BF16)

            gcol = gcol_ref[bi]
            grow = grow_ref[bi, 0]
            logi_c, logf_c = gcol[:, h:h + 1], gcol[:, NH_M + h:NH_M + h + 1]
            logi_r, logf_r = grow[h:h + 1, :], grow[NH_M + h:NH_M + h + 1, :]
            b_c = jnp.sum(jnp.where(tril, logf_r, 0.0), axis=1, keepdims=True)
            b_r = jnp.sum(jnp.where(ti <= si, logf_c, 0.0), axis=0, keepdims=True)
            m_prev = m_ref[bi, h:h + 1, 0:1]
            n_prev = n_ref[bi, h:h + 1, :]
            c_prev = c_ref[bi, h]

            d = jnp.where(tril, b_c - b_r + logi_r, -jnp.inf)
            inter = b_c + m_prev
            m_t = jnp.maximum(jnp.max(d, axis=1, keepdims=True), inter)
            w_intra = jnp.exp(d - m_t)
            w_inter = jnp.exp(inter - m_t)
            s = _dot_nt(qb, kb) * w_intra
            num = _dot(s.astype(BF16), vb) + w_inter * _dot_nt(qb, c_prev.astype(BF16))
            den = jnp.sum(s, axis=1, keepdims=True) + w_inter * jnp.sum(q * n_prev, axis=1, keepdims=True)
            hh = num / jnp.maximum(jnp.abs(den), jnp.exp(-m_t))

            m_new = m_t[L - 1:L, :]
            b_last = b_c[L - 1:L, :]
            w_state_r = jnp.exp(b_last - b_r + logi_r - m_new)
            w_state_c = jnp.exp(b_last - b_c + logi_c - m_new)
            decay = jnp.exp(b_last + m_prev - m_new)
            c_ref[bi, h] = decay * c_prev + _dot((vt * w_state_r).astype(BF16), kb)
            n_ref[bi, h:h + 1, :] = decay * n_prev + jnp.sum(k * w_state_c, axis=0, keepdims=True)
            m_ref[bi, h:h + 1, :] = jnp.broadcast_to(m_new, (1, LANES))

            hn = hh * lax.rsqrt(jnp.mean(hh * hh, axis=1, keepdims=True) + EPS) * norm_ref[:, hs]
            hm_ref[bi, :, hs] = (hn * om_ref[bi, :, hs]).astype(BF16)


def _mlstm(xm, om, gcol, grow, buf0, c0, n0, m0, w, *, chunk, nb):
    B, T, _ = xm.shape
    L = chunk
    nc = T // L
    shared = c0.shape[0] == 1 and B > 1
    nb0 = 1 if shared else nb
    st = (lambda b, c: (0, 0, 0)) if shared else (lambda b, c: (b, 0, 0))
    st4 = (lambda b, c: (0, 0, 0, 0)) if shared else (lambda b, c: (b, 0, 0, 0))
    seq = pl.BlockSpec((nb, L, D_MODEL), lambda b, c: (b, c, 0))
    wsq = _full((NH_M, DH_M, DH_M))
    return pl.pallas_call(
        functools.partial(_mlstm_body, chunk=L, shared_init=shared),
        grid=(B // nb, nc),
        in_specs=[
            seq,
            pl.BlockSpec((nb, SUBLANES, D_MODEL), lambda b, c: (b, jnp.maximum(c * (L // SUBLANES) - 1, 0), 0)),
            pl.BlockSpec((nb0, SUBLANES, D_MODEL), st),
            seq,
            pl.BlockSpec((nb, L, LANES), lambda b, c: (b, c, 0)),
            pl.BlockSpec((nb, 1, SUBLANES, L), lambda b, c: (b, c, 0, 0)),
            _full((CONV_W, D_MODEL)), _full((1, D_MODEL)),
            wsq, wsq, wsq, wsq, _full((1, D_MODEL)),
            pl.BlockSpec((nb0, NH_M, DH_M, DH_M), st4),
            pl.BlockSpec((nb0, SUBLANES, DH_M), st),
            pl.BlockSpec((nb0, SUBLANES, LANES), st),
        ],
        out_specs=(
            seq,
            pl.BlockSpec((nb, NH_M, DH_M, DH_M), lambda b, c: (b, 0, 0, 0)),
            pl.BlockSpec((nb, SUBLANES, DH_M), lambda b, c: (b, 0, 0)),
            pl.BlockSpec((nb, SUBLANES, LANES), lambda b, c: (b, 0, 0)),
        ),
        out_shape=(
            jax.ShapeDtypeStruct((B, T, D_MODEL), BF16),
            jax.ShapeDtypeStruct((B, NH_M, DH_M, DH_M), F32),
            jax.ShapeDtypeStruct((B, SUBLANES, DH_M), F32),
            jax.ShapeDtypeStruct((B, SUBLANES, LANES), F32),
        ),
        scratch_shapes=[pltpu.VMEM((nb, SUBLANES + L, D_MODEL), F32)],
        compiler_params=_cparams(("parallel", "arbitrary")),
        name="mlstm",
    )(xm, xm, buf0, om, gcol, grow, w["conv_w"], w["conv_b"], w["mq"], w["mk"], w["mv"], w["mvt"],
      w["mnorm"], c0, n0, m0)


def _lam(lq1_ref, lk1_ref, lq2_ref, lk2_ref):
    return (jnp.exp(jnp.sum(lq1_ref[...] * lk1_ref[...], axis=1, keepdims=True))
            - jnp.exp(jnp.sum(lq2_ref[...] * lk2_ref[...], axis=1, keepdims=True)) + LAMBDA_INIT)


def _online_softmax_step(m_ref, l_ref, acc_ref, s, vb):
    m_old = m_ref[...]
    m_new = jnp.maximum(m_old, jnp.max(s, axis=1, keepdims=True))
    pr = jnp.exp2(s - m_new)
    r = jnp.exp2(m_old - m_new)
    l_ref[...] = r * l_ref[...] + jnp.sum(pr, axis=1, keepdims=True)
    acc_ref[...] = r * acc_ref[...] + _dot(pr.astype(BF16), vb)
    m_ref[...] = m_new


ATTN_TQ = 1024
ONES_ROWS = 16


def _attn_prompt_body(q_ref, k_ref, v_ref, kp_ref, vp_ref, vpt_ref, lq1_ref, lk1_ref, lq2_ref, lk2_ref, subln_ref,
                      o_ref, kf_ref, vf_ref, kb_ref, vt_ref, m_ref, acc_ref, sa_ref, sb_ref, ma_ref, mb_ref, *, tq, tk,
                      has_prefix):
    qi = pl.program_id(2)
    nblk = vt_ref.shape[0]

    def with_ones(vt):
        keys = vt.shape[1]
        row = lax.broadcasted_iota(jnp.int32, (ONES_ROWS, keys), 0)
        return jnp.concatenate([vt, jnp.where(row == 0, 1.0, 0.0)], axis=0).astype(BF16)

    @pl.when(qi == 0)
    def _():
        npre = kp_ref.shape[0]
        kf_ref[0, 0:npre, :] = kp_ref[...]
        vf_ref[0, 0:npre, :] = vp_ref[...]
        kf_ref[0, npre:, :] = k_ref[0]
        vf_ref[0, npre:, :] = v_ref[0]
        kb_ref[...] = k_ref[0].astype(BF16)
        for c in range(nblk):
            vt_ref[c] = with_ones(v_ref[0, c * tk:(c + 1) * tk, :].T)

    q = q_ref[0].astype(F32)
    lane = lax.broadcasted_iota(jnp.int32, q.shape, 1)
    qq = jnp.concatenate([jnp.where(lane < DH_D, q, 0.0), jnp.where(lane < DH_D, 0.0, q)], axis=0)
    qqt = qq.T.astype(BF16)
    if has_prefix:
        st = _dot(kp_ref[...].astype(BF16), qqt)
        mx = jnp.max(st, axis=0, keepdims=True)
        m_ref[...] = mx
        acc_ref[...] = _dot(with_ones(vpt_ref[...]), jnp.exp2(st - mx).astype(BF16))
    else:
        m_ref[...] = jnp.full(m_ref.shape, -jnp.inf, F32)
        acc_ref[...] = jnp.zeros(acc_ref.shape, F32)

    def apply(st, mx, vt):
        m_old = m_ref[...]
        m_new = jnp.maximum(m_old, mx)
        acc_ref[...] = jnp.exp2(m_old - m_new) * acc_ref[...] + _dot(vt, jnp.exp2(st - m_new).astype(BF16))
        m_ref[...] = m_new

    def step(buf, j):
        s_ref, mx_ref = buf
        apply(s_ref[...], mx_ref[...], vt_ref[j])

    rel = (lax.broadcasted_iota(jnp.int32, (tk, 2 * tq), 0)
           - lax.broadcasted_iota(jnp.int32, (tk, 2 * tq), 1) % tq)

    def put(buf, j, masked):
        s_ref, mx_ref = buf
        st = _dot(kb_ref[pl.ds(pl.multiple_of(j * tk, tk), tk), :], qqt)
        if masked:
            st = jnp.where(rel <= qi * tq - j * tk, st, -jnp.inf)
        s_ref[...] = st
        mx_ref[...] = jnp.max(st, axis=0, keepdims=True)

    assert tq == 2 * tk
    buf_a, buf_b = (sa_ref, ma_ref), (sb_ref, mb_ref)
    put(buf_a, 0, True)

    def pair(p, carry):
        j = 2 * p
        put(buf_b, j + 1, False)
        step(buf_a, j)
        put(buf_a, j + 2, False)
        step(buf_b, j + 1)
        return carry

    lax.fori_loop(0, qi - 1, pair, 0)

    @pl.when(qi >= 1)
    def _():
        j = 2 * (qi - 1)
        put(buf_b, j + 1, False)
        step(buf_a, j)
        put(buf_a, j + 2, True)
        step(buf_b, j + 1)

    half = tq // 2
    cols = (slice(half, tq), slice(tq + half, 2 * tq))
    last = 2 * qi + 1
    qh = jnp.concatenate([qqt[:, c] for c in cols], axis=1)
    st = _dot(kb_ref[pl.ds(pl.multiple_of(last * tk, tk), tk), :], qh)
    tri = (lax.broadcasted_iota(jnp.int32, (tk, tq), 0)
           <= lax.broadcasted_iota(jnp.int32, (tk, tq), 1) % half)
    st = jnp.where(tri, st, -jnp.inf)
    mx = jnp.max(st, axis=0, keepdims=True)
    step(buf_a, 2 * qi)
    m_old = jnp.concatenate([m_ref[:, c] for c in cols], axis=1)
    m_new = jnp.maximum(m_old, mx)
    r = jnp.exp2(m_old - m_new)
    pv = _dot(vt_ref[last], jnp.exp2(st - m_new).astype(BF16))
    for i, c in enumerate(cols):
        part = slice(i * half, (i + 1) * half)
        acc_ref[:, c] = r[:, part] * acc_ref[:, c] + pv[:, part]
        m_ref[:, c] = m_new[:, part]

    lam = _lam(lq1_ref, lk1_ref, lq2_ref, lk2_ref)
    acc = acc_ref[0:DV_D, :] / acc_ref[DV_D:DV_D + 1, :]
    ot = acc[:, 0:tq] - lam * acc[:, tq:2 * tq]
    ot = ot * lax.rsqrt(jnp.mean(ot * ot, axis=0, keepdims=True) + EPS) * subln_ref[...]
    o_ref[0] = (ot * (1.0 - LAMBDA_INIT)).T.astype(BF16)


def _attn_prompt(q, k, v, kpre, vpre, lams, subln_col, *, tq, has_prefix):
    vpre_t = vpre.T
    B, T, _ = q.shape
    P = kpre.shape[0]
    tk = tq // 2
    assert T % tq == 0 and tq % (2 * tk) == 0
    lspec = _full((1, DH_D))
    return pl.pallas_call(
        functools.partial(_attn_prompt_body, tq=tq, tk=tk, has_prefix=has_prefix),
        grid=(B, NH_D, T // tq),
        in_specs=[
            pl.BlockSpec((1, tq, LANES), lambda b, h, i: (b, i, h)),
            pl.BlockSpec((1, T, LANES), lambda b, h, i: (b, 0, h)),
            pl.BlockSpec((1, T, LANES), lambda b, h, i: (b, 0, h)),
            pl.BlockSpec((P, LANES), lambda b, h, i: (0, h)),
            pl.BlockSpec((P, LANES), lambda b, h, i: (0, h)),
            pl.BlockSpec((DV_D, P), lambda b, h, i: (h, 0)),
            lspec, lspec, lspec, lspec, _full((DV_D, 1)),
        ],
        out_specs=(pl.BlockSpec((1, tq, LANES), lambda b, h, i: (b, i, h)),
                   pl.BlockSpec((1, P + T, LANES), lambda b, h, i: (b, 0, h)),
                   pl.BlockSpec((1, P + T, LANES), lambda b, h, i: (b, 0, h))),
        out_shape=(jax.ShapeDtypeStruct((B, T, D_MODEL), BF16),
                   jax.ShapeDtypeStruct((B, P + T, D_MODEL), F32),
                   jax.ShapeDtypeStruct((B, P + T, D_MODEL), F32)),
        scratch_shapes=[pltpu.VMEM((T, LANES), BF16), pltpu.VMEM((T // tk, DV_D + ONES_ROWS, tk), BF16),
                        pltpu.VMEM((1, 2 * tq), F32), pltpu.VMEM((DV_D + ONES_ROWS, 2 * tq), F32),
                        pltpu.VMEM((tk, 2 * tq), F32), pltpu.VMEM((tk, 2 * tq), F32),
                        pltpu.VMEM((1, 2 * tq), F32), pltpu.VMEM((1, 2 * tq), F32)],
        compiler_params=_cparams(("parallel", "parallel", "arbitrary")),
        name="attn_prompt",
    )(q, k, v, kpre, vpre, vpre_t, *lams, subln_col)


QROWS = 8
QSUB = 4
QMAT_ROWS = 2 * NH_D * QSUB
PAGES_PER_STEP = 16


def _attn_sample_body(pt_ref, q_ref, *refs, n_new, n_pg):
    kc_refs, vc_refs = refs[:n_pg], refs[n_pg:2 * n_pg]
    (kn_ref, vn_ref, lq1_ref, lk1_ref, lq2_ref, lk2_ref, subln_ref,
     o_ref, qm_ref, bias_ref, m_ref, l_ref, acc_ref) = refs[2 * n_pg:]
    p = pl.program_id(1)
    R = QMAT_ROWS
    rows = kc_refs[0].shape[1]
    half = R // 2

    def head_of(r):
        return (r % half) // QSUB

    @pl.when(p == 0)
    def _():
        q = q_ref[0].astype(F32)
        sub = lax.broadcasted_iota(jnp.int32, (QROWS, LANES), 0)
        pieces = []
        for j in range(NH_D // 2):
            even = q[:, (2 * j) * LANES:(2 * j + 1) * LANES]
            odd = q[:, (2 * j + 1) * LANES:(2 * j + 2) * LANES]
            pieces.append(jnp.where(sub < QSUB, even, pltpu.roll(odd, QSUB, axis=0)))
        qh = jnp.concatenate(pieces, axis=0)
        lane = lax.broadcasted_iota(jnp.int32, qh.shape, 1)
        qm_ref[...] = jnp.concatenate([jnp.where(lane < DH_D, qh, 0.0), jnp.where(lane < DH_D, 0.0, qh)],
                                      axis=0).astype(BF16)
        row = lax.broadcasted_iota(jnp.int32, (R, rows), 0)
        col = lax.broadcasted_iota(jnp.int32, (R, rows), 1)
        bias_ref[...] = jnp.where(col % NH_D == head_of(row), 0.0, -jnp.inf)
        m_ref[...] = jnp.full(m_ref.shape, -jnp.inf, F32)
        l_ref[...] = jnp.zeros(l_ref.shape, F32)
        acc_ref[...] = jnp.zeros(acc_ref.shape, F32)

    qm = qm_ref[...]
    bias = bias_ref[...]
    s = [_dot_nt(qm, kc[0].astype(BF16)) + bias for kc in kc_refs]
    m_old = m_ref[...]
    m_new = m_old
    for si in s:
        m_new = jnp.maximum(m_new, jnp.max(si, axis=1, keepdims=True))
    r = jnp.exp2(m_old - m_new)
    l_new = r * l_ref[...]
    acc = r * acc_ref[...]
    for si, vc in zip(s, vc_refs):
        pr = jnp.exp2(si - m_new)
        l_new = l_new + jnp.sum(pr, axis=1, keepdims=True)
        acc = acc + _dot(pr.astype(BF16), vc[0].astype(BF16))
    m_ref[...] = m_new
    l_ref[...] = l_new
    acc_ref[...] = acc

    @pl.when(p == pl.num_programs(1) - 1)
    def _():
        zpad = jnp.zeros((LANES - kn_ref.shape[1], LANES), F32)
        kn = jnp.concatenate([kn_ref[0], zpad], axis=0).astype(BF16)
        vn = jnp.concatenate([vn_ref[0], zpad], axis=0).astype(BF16)
        sn = _dot_nt(qm, kn)
        row = lax.broadcasted_iota(jnp.int32, sn.shape, 0)
        col = lax.broadcasted_iota(jnp.int32, sn.shape, 1)
        tok = col // NH_D
        ok = (col % NH_D == head_of(row)) & (tok <= row % QSUB) & (tok < n_new)
        _online_softmax_step(m_ref, l_ref, acc_ref, jnp.where(ok, sn, -jnp.inf), vn)
        lam = _lam(lq1_ref, lk1_ref, lq2_ref, lk2_ref)
        o = acc_ref[0:half, :] / l_ref[0:half, :] - lam * (acc_ref[half:R, :] / l_ref[half:R, :])
        o_ref[0] = (_rms(o, subln_ref[...]) * (1.0 - LAMBDA_INIT)).astype(BF16)


def _attn_sample(page_table, q, cache_k, cache_v, knew, vnew, lams, subln, *, n_new):
    B, n_pages = page_table.shape
    rows = cache_k.shape[1]
    R = QMAT_ROWS
    n_pg = math.gcd(n_pages, PAGES_PER_STEP)
    lspec = pl.BlockSpec((1, DH_D), lambda b, p, pt: (0, 0))
    pgs = [pl.BlockSpec((1, rows, LANES), lambda b, p, pt, i=i: (pt[b, p * n_pg + i], 0, 0)) for i in range(n_pg)]
    new = pl.BlockSpec((1, n_new * NH_D, LANES), lambda b, p, pt: (b, 0, 0))
    return pl.pallas_call(
        functools.partial(_attn_sample_body, n_new=n_new, n_pg=n_pg),
        grid_spec=pltpu.PrefetchScalarGridSpec(
            num_scalar_prefetch=1,
            grid=(B, n_pages // n_pg),
            in_specs=[pl.BlockSpec((1, QROWS, D_MODEL), lambda b, p, pt: (b, 0, 0))] + pgs + pgs
                     + [new, new, lspec, lspec, lspec, lspec, pl.BlockSpec((1, DV_D), lambda b, p, pt: (0, 0))],
            out_specs=pl.BlockSpec((1, R // 2, DV_D), lambda b, p, pt: (b, 0, 0)),
            scratch_shapes=[pltpu.VMEM((R, LANES), BF16), pltpu.VMEM((R, rows), F32), pltpu.VMEM((R, 1), F32),
                            pltpu.VMEM((R, 1), F32), pltpu.VMEM((R, DV_D), F32)],
        ),
        out_shape=jax.ShapeDtypeStruct((B, R // 2, DV_D), BF16),
        compiler_params=_cparams(("parallel", "arbitrary")),
        name="attn_sample",
    )(page_table, q, *([cache_k] * n_pg), *([cache_v] * n_pg), knew, vnew, *lams, subln)


def _merge_body(hm_ref, o_ref, ga_ref, gb_ref, x_ref, wa_ref, wb_ref, wo_ref, y_ref):
    ya = _dot(hm_ref[...], wa_ref[...])
    yb = _dot(o_ref[...], wb_ref[...])
    merged = _sigmoid(ga_ref[...]) * ya + _sigmoid(gb_ref[...]) * yb
    y_ref[...] = x_ref[...] + _dot(merged.astype(BF16), wo_ref[...])


def _merge(hm, o, ga, gb, x, wa, wb, wo, *, tm=1024):
    n = hm.shape[0]
    tm = _row_tile(n, tm)
    row = pl.BlockSpec((tm, D_MODEL), lambda i: (i, 0))
    wspec = _resident((D_MODEL, D_MODEL))
    return pl.pallas_call(
        _merge_body,
        grid=(n // tm,),
        in_specs=[row, row, row, row, row, wspec, wspec, wspec],
        out_specs=row,
        out_shape=jax.ShapeDtypeStruct((n, D_MODEL), F32),
        compiler_params=_cparams(("parallel",)),
        name="merge",
    )(hm, o, ga, gb, x, wa, wb, wo)


def _rope_tables(pos):
    half = ROT_DIM // 2
    inv_freq = ROPE_THETA ** (-jnp.arange(half, dtype=F32) / half)
    d = jnp.arange(LANES, dtype=jnp.int32) % DH_D
    freq = jnp.where(d < ROT_DIM, inv_freq[d % half], 0.0)
    ang = pos.astype(F32)[:, None] * freq[None, :]
    c, s = jnp.cos(ang), jnp.sin(ang)
    s_up = jnp.where(d < half, -s, 0.0)
    s_dn = jnp.where(d >= half, s, 0.0)
    return c, s_up, s_dn


def _pad_rows(a, rows, axis, front=False):
    pad = [(0, 0)] * a.ndim
    pad[axis] = (rows - a.shape[axis], 0) if front else (0, rows - a.shape[axis])
    return jnp.pad(a, pad)


def kernel(x_prompt, x_sample, cache_k, cache_v, page_table, state_C, state_n, state_m, state_conv, meta_tokens, ffn1_norm, ffn1_w_gate, ffn1_w_up, ffn1_w_down, mix_norm, w_in, conv_w, conv_b, w_mq, w_mk, w_mv, b_igate, b_fgate, mlstm_norm, lambda_q1, lambda_k1, lambda_q2, lambda_k2, subln, w_proj_a, w_proj_b, w_out, ffn2_norm, ffn2_w_gate, ffn2_w_up, ffn2_w_down, final_norm):
    bp, seq, _ = x_prompt.shape
    bs, ts, _ = x_sample.shape
    n_pages = page_table.shape[1]
    page = cache_k.shape[2]
    past_len = n_pages * page
    assert w_in.shape[0] == 1 and ts <= QSUB

    win = w_in[0]
    offs = [0]
    for width in (D_MODEL, D_MODEL, NH_M, NH_M, D_MODEL, D_MODEL, D_MODEL, D_MODEL, D_MODEL):
        offs.append(offs[-1] + width)
    col = lambda i: win[:, offs[i]:offs[i + 1]]
    wgate = jnp.pad(jnp.concatenate([col(2), col(3)], axis=1), ((0, 0), (0, LANES - 2 * NH_M)))
    wproj = {"xm": col(0).astype(BF16), "om": col(1).astype(BF16), "q": col(4).astype(BF16),
             "k": col(5).astype(BF16), "v": col(6).astype(BF16), "ga": col(7).astype(BF16),
             "gb": col(8).astype(BF16), "gate": wgate.astype(BF16)}
    gbias = jnp.pad(jnp.concatenate([b_igate[0], b_fgate[0]])[None, :], ((0, 0), (0, LANES - 2 * NH_M)))
    wm = {"conv_w": conv_w[0], "conv_b": conv_b[0][None, :],
          "mq": w_mq[0].astype(BF16), "mk": (w_mk[0] * DH_M ** -0.5).astype(BF16), "mv": w_mv[0].astype(BF16),
          "mvt": jnp.swapaxes(w_mv[0], 1, 2).astype(BF16), "mnorm": mlstm_norm[0].reshape(1, D_MODEL)}
    lams = (lambda_q1, lambda_k1, lambda_q2, lambda_k2)
    row = lambda a: a.reshape(1, -1)
    f1 = (row(ffn1_norm[0]), ffn1_w_gate[0].astype(BF16), ffn1_w_up[0].astype(BF16), ffn1_w_down[0].astype(BF16))
    f2 = (row(ffn2_norm[0]), ffn2_w_gate[0].astype(BF16), ffn2_w_up[0].astype(BF16), ffn2_w_down[0].astype(BF16))
    fin = row(final_norm)
    wa, wb, wo = w_proj_a[0].astype(BF16), w_proj_b[0].astype(BF16), w_out[0].astype(BF16)

    n_main = bp * seq
    n_samp = bs * ts
    x_main = x_prompt.reshape(n_main, D_MODEL)
    x_small = jnp.concatenate([x_sample.reshape(n_samp, D_MODEL), meta_tokens], axis=0)

    tm_in = min(seq, 512)
    tabs_main = _rope_tables(N_META + jnp.arange(seq, dtype=jnp.int32))
    pos_small = jnp.concatenate([jnp.tile(past_len + jnp.arange(ts, dtype=jnp.int32), bs),
                                 jnp.arange(N_META, dtype=jnp.int32)])
    tabs_small = _rope_tables(pos_small)

    x1_main = _ffn(x_main, *f1, fin, final_norm=False)
    x1_small = _ffn(x_small, *f1, fin, final_norm=False)
    pm = _inproj(x1_main, row(mix_norm[0]), wproj, gbias, tabs_main, n_rope_blocks=seq // tm_in, tm=tm_in)
    ps = _inproj(x1_small, row(mix_norm[0]), wproj, gbias, tabs_small, n_rope_blocks=1, tm=x_small.shape[0])
    xm_m, om_m, q_m, k_m, v_m, ga_m, gb_m, gcol_m, grow_m = pm
    xm_s, om_s, q_s, k_s, v_s, ga_s, gb_s, gcol_s, grow_s = ps

    def split(a):
        return a[:n_samp], a[n_samp:]

    xm_smp, xm_meta = split(xm_s)
    om_smp, om_meta = split(om_s)
    gcol_smp, gcol_meta = split(gcol_s)
    grow_smp, grow_meta = grow_s[:, :n_samp], grow_s[:, n_samp:]

    zc = jnp.zeros((1, NH_M, DH_M, DH_M), F32)
    zn = jnp.zeros((1, SUBLANES, DH_M), F32)
    zm = jnp.zeros((1, SUBLANES, LANES), F32)
    zbuf = jnp.zeros((1, SUBLANES, D_MODEL), F32)
    hm_meta, c_meta, n_meta, m_meta = _mlstm(
        xm_meta[None], om_meta[None], gcol_meta[None], grow_meta.reshape(1, 1, SUBLANES, N_META),
        zbuf, zc, zn, zm, wm, chunk=N_META, nb=1)

    chunk = _row_tile(seq, MLSTM_CHUNK)
    grow_main = grow_m.reshape(SUBLANES, bp, seq // chunk, chunk).transpose(1, 2, 0, 3)
    buf_main = _pad_rows(xm_meta[N_META - (CONV_W - 1):], SUBLANES, 0, front=True)[None]
    hm_main, c_main, n_main_st, m_main = _mlstm(
        xm_m.reshape(bp, seq, D_MODEL), om_m.reshape(bp, seq, D_MODEL), gcol_m.reshape(bp, seq, LANES),
        grow_main, buf_main, c_meta, n_meta, m_meta, wm, chunk=chunk, nb=_group(bp))

    pad_t = lambda a: _pad_rows(a.reshape(bs, ts, a.shape[-1]), QROWS, 1)
    gcol_pad = jnp.concatenate(
        [gcol_smp.reshape(bs, ts, LANES),
         jnp.broadcast_to(jnp.where(jnp.arange(LANES) < NH_M, NEG_BIG, 0.0).astype(F32), (bs, QROWS - ts, LANES))],
        axis=1)
    grow_pad = jnp.concatenate(
        [grow_smp.reshape(SUBLANES, bs, ts),
         jnp.broadcast_to(jnp.where(jnp.arange(SUBLANES) < NH_M, NEG_BIG, 0.0).astype(F32)[:, None, None],
                          (SUBLANES, bs, QROWS - ts))], axis=2).transpose(1, 0, 2)[:, None]
    hm_smp, c_smp, n_smp, m_smp = _mlstm(
        pad_t(xm_smp), pad_t(om_smp), gcol_pad, grow_pad,
        _pad_rows(state_conv[0], SUBLANES, 1, front=True), state_C[0],
        _pad_rows(state_n[0], SUBLANES, 1), jnp.broadcast_to(_pad_rows(state_m[0], SUBLANES, 1)[:, :, None],
                                                             (bs, SUBLANES, LANES)),
        wm, chunk=QROWS, nb=_group(bs))

    q_smp, q_meta = split(q_s)
    k_smp, k_meta = split(k_s)
    v_smp, v_meta = split(v_s)
    sub = row(subln[0])
    sub_col = subln[0].reshape(DV_D, 1)
    o_meta = _attn_prompt(q_meta[None], k_meta[None], v_meta[None], k_meta, v_meta, lams, sub_col,
                          tq=N_META, has_prefix=False)[0]
    tq = min(seq, ATTN_TQ)
    o_main, k_full, v_full = _attn_prompt(q_m.reshape(bp, seq, D_MODEL), k_m.reshape(bp, seq, D_MODEL),
                                          v_m.reshape(bp, seq, D_MODEL), k_meta, v_meta, lams, sub_col,
                                          tq=tq, has_prefix=True)
    del o_meta
    k_new = k_smp.reshape(1, bs, ts, NH_D, DV_D)
    v_new = v_smp.reshape(1, bs, ts, NH_D, DV_D)
    o_smp = _attn_sample(page_table, pad_t(q_smp), cache_k.reshape(-1, page * NH_D, DV_D),
                         cache_v.reshape(-1, page * NH_D, DV_D), k_new.reshape(bs, ts * NH_D, DV_D),
                         v_new.reshape(bs, ts * NH_D, DV_D), lams, sub, n_new=ts)
    o_smp = o_smp.reshape(bs, NH_D, QSUB, DV_D)[:, :, :ts].transpose(0, 2, 1, 3).reshape(n_samp, D_MODEL)

    x2_main = _merge(hm_main.reshape(n_main, D_MODEL), o_main.reshape(n_main, D_MODEL), ga_m, gb_m, x1_main,
                     wa, wb, wo)
    y_main = _ffn(x2_main, *f2, fin, final_norm=True)
    x2_smp = _merge(hm_smp[:, :ts].reshape(n_samp, D_MODEL), o_smp,
                    ga_s, gb_s, x1_small, wa, wb, wo)
    y_smp = _ffn(x2_smp, *f2, fin, final_norm=True)
    del hm_meta

    xm_main3 = xm_m.reshape(bp, seq, D_MODEL)
    return (
        y_main.reshape(bp, seq, D_MODEL),
        y_smp.reshape(bs, ts, D_MODEL),
        k_full.reshape(1, bp, seq + N_META, NH_D, DV_D),
        v_full.reshape(1, bp, seq + N_META, NH_D, DV_D),
        c_main[None],
        n_main_st[None, :, :NH_M, :],
        m_main[None, :, :NH_M, 0],
        xm_main3[None, :, seq - (CONV_W - 1):, :],
        k_new,
        v_new,
        c_smp[None],
        n_smp[None, :, :NH_M, :],
        m_smp[None, :, :NH_M, 0],
        xm_smp.reshape(bs, ts, D_MODEL)[None, :, ts - (CONV_W - 1):, :],
    )
```

```python
import functools
import math

import jax
import jax.numpy as jnp
from jax import lax
from jax.experimental import pallas as pl
from jax.experimental.pallas import tpu as pltpu

F32 = jnp.float32
BF16 = jnp.bfloat16

D_MODEL = 1024
N_META = 16
EPS = 1e-6
D_FF = 2816
NH_M = 4
DH_M = 256
CONV_W = 4
NH_D = 8
DH_D = 64
DV_D = 128
ROT_DIM = 16
ROPE_THETA = 500000.0
LAMBDA_INIT = 0.8 - 0.6 * math.exp(-0.3 * 0)

LANES = 128
SUBLANES = 8
NEG_BIG = -1e30
V7X_VMEM_BYTES = 64 * 1024 * 1024
VMEM_LIMIT = V7X_VMEM_BYTES - 8 * 1024 * 1024


def _cparams(sem):
    return pltpu.CompilerParams(dimension_semantics=sem, vmem_limit_bytes=VMEM_LIMIT)


def _rms(x, g):
    return x * lax.rsqrt(jnp.mean(x * x, axis=-1, keepdims=True) + EPS) * g


def _sigmoid(x):
    return 1.0 / (1.0 + jnp.exp(-x))


def _dot(a, b):
    return jnp.dot(a, b, preferred_element_type=F32)


def _dot_nt(a, b):
    return lax.dot_general(a, b, (((1,), (1,)), ((), ())), preferred_element_type=F32)


def _row_tile(n, want):
    if n <= 2 * want:
        return n
    t = want
    while n % t:
        t //= 2
    assert t == n or t % SUBLANES == 0
    return t


MLSTM_CHUNK = 256
MLSTM_GROUP = 8


def _group(n):
    return max(g for g in range(1, MLSTM_GROUP + 1) if n % g == 0)


def _full(shape):
    zeros = (0,) * len(shape)
    return pl.BlockSpec(shape, lambda *_: zeros)


def _resident(shape):
    zeros = (0,) * len(shape)
    return pl.BlockSpec(shape, lambda *_: zeros, pipeline_mode=pl.Buffered(1))


def _ffn_body(x_ref, g_ref, wg_ref, wu_ref, wd_ref, fg_ref, o_ref, *, final_norm):
    x = x_ref[...]
    xb = _rms(x, g_ref[...]).astype(BF16)
    g = _dot(xb, wg_ref[...])
    u = _dot(xb, wu_ref[...])
    a = (g * _sigmoid(g) * u).astype(BF16)
    y = x + 0.5 * _dot(a, wd_ref[...])
    if final_norm:
        y = _rms(y, fg_ref[...])
    o_ref[...] = y


def _ffn(x, gain, wg, wu, wd, final_gain, *, final_norm, tm=1024):
    n = x.shape[0]
    tm = _row_tile(n, tm)
    row = pl.BlockSpec((tm, D_MODEL), lambda i: (i, 0))
    return pl.pallas_call(
        functools.partial(_ffn_body, final_norm=final_norm),
        grid=(n // tm,),
        in_specs=[row, _full((1, D_MODEL)), _resident(wg.shape), _resident(wu.shape), _resident(wd.shape),
                  _full((1, D_MODEL))],
        out_specs=row,
        out_shape=jax.ShapeDtypeStruct((n, D_MODEL), F32),
        compiler_params=_cparams(("parallel",)),
        name="ffn_final" if final_norm else "ffn",
    )(x, gain, wg, wu, wd, final_gain)


def _rope_slab(xs, c, s_up, s_dn):
    return xs * c + pltpu.roll(xs, LANES - ROT_DIM // 2, axis=1) * s_up + pltpu.roll(xs, ROT_DIM // 2, axis=1) * s_dn


def _inproj_body(x_ref, g_ref, wxm_ref, wom_ref, wq_ref, wk_ref, wv_ref, wga_ref, wgb_ref, wgate_ref,
                 gbias_ref, cq_ref, cu_ref, cd_ref,
                 xm_ref, om_ref, q_ref, k_ref, v_ref, ga_ref, gb_ref, gcol_ref, grow_ref):
    hb = _rms(x_ref[...], g_ref[...]).astype(BF16)
    xm_ref[...] = _dot(hb, wxm_ref[...])
    om_ref[...] = _sigmoid(_dot(hb, wom_ref[...]))
    v_ref[...] = _dot(hb, wv_ref[...])
    ga_ref[...] = _dot(hb, wga_ref[...])
    gb_ref[...] = _dot(hb, wgb_ref[...])
    c, s_up, s_dn = cq_ref[...], cu_ref[...], cd_ref[...]
    q = _dot(hb, wq_ref[...])
    k = _dot(hb, wk_ref[...])
    qscale = DH_D ** -0.5 * math.log2(math.e)
    for h in range(NH_D):
        sl = slice(h * LANES, (h + 1) * LANES)
        q_ref[:, sl] = (_rope_slab(q[:, sl], c, s_up, s_dn) * qscale).astype(BF16)
        k_ref[:, sl] = _rope_slab(k[:, sl], c, s_up, s_dn)
    pre = _dot(hb, wgate_ref[...]) + gbias_ref[...]
    lane = lax.broadcasted_iota(jnp.int32, pre.shape, 1)
    logsig = jnp.minimum(pre, 0.0) - jnp.log(1.0 + jnp.exp(-jnp.abs(pre)))
    glog = jnp.where(lane < NH_M, pre, logsig)
    gcol_ref[...] = glog
    grow_ref[...] = glog.T[:SUBLANES, :]


def _inproj(x, gain, w, gbias, rope_tabs, *, n_rope_blocks, tm):
    n = x.shape[0]
    assert n % tm == 0
    row = pl.BlockSpec((tm, D_MODEL), lambda i: (i, 0))
    tab = pl.BlockSpec((tm, LANES), lambda i: (i % n_rope_blocks, 0))
    wspec = _resident((D_MODEL, D_MODEL))
    out_shapes = (
        jax.ShapeDtypeStruct((n, D_MODEL), F32),
        jax.ShapeDtypeStruct((n, D_MODEL), F32),
        jax.ShapeDtypeStruct((n, D_MODEL), BF16),
        jax.ShapeDtypeStruct((n, D_MODEL), F32),
        jax.ShapeDtypeStruct((n, D_MODEL), F32),
        jax.ShapeDtypeStruct((n, D_MODEL), F32),
        jax.ShapeDtypeStruct((n, D_MODEL), F32),
        jax.ShapeDtypeStruct((n, LANES), F32),
        jax.ShapeDtypeStruct((SUBLANES, n), F32),
    )
    return pl.pallas_call(
        _inproj_body,
        grid=(n // tm,),
        in_specs=[row, _full((1, D_MODEL))] + [wspec] * 7 + [_full((D_MODEL, LANES)), _full((1, LANES)),
                                                              tab, tab, tab],
        out_specs=(row, row, row, row, row, row, row,
                   pl.BlockSpec((tm, LANES), lambda i: (i, 0)),
                   pl.BlockSpec((SUBLANES, tm), lambda i: (0, i))),
        out_shape=out_shapes,
        compiler_params=_cparams(("parallel",)),
        name="inproj",
    )(x, gain, w["xm"], w["om"], w["q"], w["k"], w["v"], w["ga"], w["gb"], w["gate"], gbias, *rope_tabs)


def _mlstm_body(xm_ref, xprev_ref, buf0_ref, om_ref, gcol_ref, grow_ref, cw_ref, cb_ref,
                wq_ref, wk_ref, wv_ref, wvt_ref, norm_ref, c0_ref, n0_ref, m0_ref,
                hm_ref, c_ref, n_ref, m_ref, ext_ref, *, chunk, shared_init):
    ci = pl.program_id(1)
    L = chunk
    nb = xm_ref.shape[0]

    @pl.when(ci == 0)
    def _():
        for bi in range(nb):
            src = 0 if shared_init else bi
            c_ref[bi] = c0_ref[src]
            n_ref[bi] = n0_ref[src]
            m_ref[bi] = m0_ref[src]
            ext_ref[bi, 0:SUBLANES, :] = buf0_ref[src]

    @pl.when(ci > 0)
    def _():
        ext_ref[:, 0:SUBLANES, :] = xprev_ref[...]

    x = xm_ref[...]
    ext_ref[:, SUBLANES:SUBLANES + L, :] = x
    e = ext_ref[...]
    cv = e * cw_ref[0:1, :]
    for j in range(1, CONV_W):
        cv = pltpu.roll(cv, 1, axis=1) + e * cw_ref[j:j + 1, :]
    cv = cv[:, SUBLANES:, :] + cb_ref[...]
    cact = cv * _sigmoid(cv)

    ti = lax.broadcasted_iota(jnp.int32, (L, L), 0)
    si = lax.broadcasted_iota(jnp.int32, (L, L), 1)
    tril = si <= ti

    for h in range(NH_M):
        hs = slice(h * DH_M, (h + 1) * DH_M)
        chb = cact[:, :, hs].reshape(nb * L, DH_M).astype(BF16)
        xhb = x[:, :, hs].reshape(nb * L, DH_M).astype(BF16)
        q_all = _dot(chb, wq_ref[h])
        k_all = _dot(chb, wk_ref[h])
        v_all = _dot(xhb, wv_ref[h])
        for bi in range(nb):
            rows = slice(bi * L, (bi + 1) * L)
            q, k, v = q_all[rows], k_all[rows], v_all[rows]
            vt = _dot_nt(wvt_ref[h], xhb[rows])
            qb, kb, vb = q.astype(BF16), k.astype(BF16), v.astype(BF16)

            gcol = gcol_ref[bi]
            grow = grow_ref[bi, 0]
            logi_c, logf_c = gcol[:, h:h + 1], gcol[:, NH_M + h:NH_M + h + 1]
            logi_r, logf_r = grow[h:h + 1, :], grow[NH_M + h:NH_M + h + 1, :]
            b_c = jnp.sum(jnp.where(tril, logf_r, 0.0), axis=1, keepdims=True)
            b_r = jnp.sum(jnp.where(ti <= si, logf_c, 0.0), axis=0, keepdims=True)
            m_prev = m_ref[bi, h:h + 1, 0:1]
            n_prev = n_ref[bi, h:h + 1, :]
            c_prev = c_ref[bi, h]

            d = jnp.where(tril, b_c - b_r + logi_r, -jnp.inf)
            inter = b_c + m_prev
            m_t = jnp.maximum(jnp.max(d, axis=1, keepdims=True), inter)
            w_intra = jnp.exp(d - m_t)
            w_inter = jnp.exp(inter - m_t)
            s = _dot_nt(qb, kb) * w_intra
            num = _dot(s.astype(BF16), vb) + w_inter * _dot_nt(qb, c_prev.astype(BF16))
            den = jnp.sum(s, axis=1, keepdims=True) + w_inter * jnp.sum(q * n_prev, axis=1, keepdims=True)
            hh = num / jnp.maximum(jnp.abs(den), jnp.exp(-m_t))

            m_new = m_t[L - 1:L, :]
            b_last = b_c[L - 1:L, :]
            w_state_r = jnp.exp(b_last - b_r + logi_r - m_new)
            w_state_c = jnp.exp(b_last - b_c + logi_c - m_new)
            decay = jnp.exp(b_last + m_prev - m_new)
            c_ref[bi, h] = decay * c_prev + _dot((vt * w_state_r).astype(BF16), kb)
            n_ref[bi, h:h + 1, :] = decay * n_prev + jnp.sum(k * w_state_c, axis=0, keepdims=True)
            m_ref[bi, h:h + 1, :] = jnp.broadcast_to(m_new, (1, LANES))

            hn = hh * lax.rsqrt(jnp.mean(hh * hh, axis=1, keepdims=True) + EPS) * norm_ref[:, hs]
            hm_ref[bi, :, hs] = (hn * om_ref[bi, :, hs]).astype(BF16)


def _mlstm(xm, om, gcol, grow, buf0, c0, n0, m0, w, *, chunk, nb):
    B, T, _ = xm.shape
    L = chunk
    nc = T // L
    shared = c0.shape[0] == 1 and B > 1
    nb0 = 1 if shared else nb
    st = (lambda b, c: (0, 0, 0)) if shared else (lambda b, c: (b, 0, 0))
    st4 = (lambda b, c: (0, 0, 0, 0)) if shared else (lambda b, c: (b, 0, 0, 0))
    seq = pl.BlockSpec((nb, L, D_MODEL), lambda b, c: (b, c, 0))
    wsq = _full((NH_M, DH_M, DH_M))
    return pl.pallas_call(
        functools.partial(_mlstm_body, chunk=L, shared_init=shared),
        grid=(B // nb, nc),
        in_specs=[
            seq,
            pl.BlockSpec((nb, SUBLANES, D_MODEL), lambda b, c: (b, jnp.maximum(c * (L // SUBLANES) - 1, 0), 0)),
            pl.BlockSpec((nb0, SUBLANES, D_MODEL), st),
            seq,
            pl.BlockSpec((nb, L, LANES), lambda b, c: (b, c, 0)),
            pl.BlockSpec((nb, 1, SUBLANES, L), lambda b, c: (b, c, 0, 0)),
            _full((CONV_W, D_MODEL)), _full((1, D_MODEL)),
            wsq, wsq, wsq, wsq, _full((1, D_MODEL)),
            pl.BlockSpec((nb0, NH_M, DH_M, DH_M), st4),
            pl.BlockSpec((nb0, SUBLANES, DH_M), st),
            pl.BlockSpec((nb0, SUBLANES, LANES), st),
        ],
        out_specs=(
            seq,
            pl.BlockSpec((nb, NH_M, DH_M, DH_M), lambda b, c: (b, 0, 0, 0)),
            pl.BlockSpec((nb, SUBLANES, DH_M), lambda b, c: (b, 0, 0)),
            pl.BlockSpec((nb, SUBLANES, LANES), lambda b, c: (b, 0, 0)),
        ),
        out_shape=(
            jax.ShapeDtypeStruct((B, T, D_MODEL), BF16),
            jax.ShapeDtypeStruct((B, NH_M, DH_M, DH_M), F32),
            jax.ShapeDtypeStruct((B, SUBLANES, DH_M), F32),
            jax.ShapeDtypeStruct((B, SUBLANES, LANES), F32),
        ),
        scratch_shapes=[pltpu.VMEM((nb, SUBLANES + L, D_MODEL), F32)],
        compiler_params=_cparams(("parallel", "arbitrary")),
        name="mlstm",
    )(xm, xm, buf0, om, gcol, grow, w["conv_w"], w["conv_b"], w["mq"], w["mk"], w["mv"], w["mvt"],
      w["mnorm"], c0, n0, m0)


def _lam(lq1_ref, lk1_ref, lq2_ref, lk2_ref):
    return (jnp.exp(jnp.sum(lq1_ref[...] * lk1_ref[...], axis=1, keepdims=True))
            - jnp.exp(jnp.sum(lq2_ref[...] * lk2_ref[...], axis=1, keepdims=True)) + LAMBDA_INIT)


def _online_softmax_step(m_ref, l_ref, acc_ref, s, vb):
    m_old = m_ref[...]
    m_new = jnp.maximum(m_old, jnp.max(s, axis=1, keepdims=True))
    pr = jnp.exp2(s - m_new)
    r = jnp.exp2(m_old - m_new)
    l_ref[...] = r * l_ref[...] + jnp.sum(pr, axis=1, keepdims=True)
    acc_ref[...] = r * acc_ref[...] + _dot(pr.astype(BF16), vb)
    m_ref[...] = m_new


ATTN_TQ = 512
ATTN_HEADS = 2
ONES_ROWS = 16


def _attn_prompt_body(q_ref, k_ref, v_ref, kp_ref, vp_ref, vpt_ref, lq1_ref, lk1_ref, lq2_ref, lk2_ref, subln_ref,
                      o_ref, kf_ref, vf_ref, kb_ref, vt_ref, m_ref, acc_ref, sa_ref, sb_ref, ma_ref, mb_ref, *, tq, tk,
                      has_prefix):
    qi = pl.program_id(2)
    nblk = vt_ref.shape[1]
    heads = range(kb_ref.shape[0])

    def lanes(e):
        return slice(e * LANES, (e + 1) * LANES)

    def with_ones(vt):
        keys = vt.shape[1]
        row = lax.broadcasted_iota(jnp.int32, (ONES_ROWS, keys), 0)
        return jnp.concatenate([vt, jnp.where(row == 0, 1.0, 0.0)], axis=0).astype(BF16)

    @pl.when(qi == 0)
    def _():
        npre = kp_ref.shape[0]
        kf_ref[0, 0:npre, :] = kp_ref[...]
        vf_ref[0, 0:npre, :] = vp_ref[...]
        kf_ref[0, npre:, :] = k_ref[0]
        vf_ref[0, npre:, :] = v_ref[0]
        for e in heads:
            kb_ref[e] = k_ref[0, :, lanes(e)].astype(BF16)
            for c in range(nblk):
                vt_ref[e, c] = with_ones(v_ref[0, c * tk:(c + 1) * tk, lanes(e)].T)

    qqts = []
    for e in heads:
        q = q_ref[0, :, lanes(e)].astype(F32)
        lane = lax.broadcasted_iota(jnp.int32, q.shape, 1)
        qq = jnp.concatenate([jnp.where(lane < DH_D, q, 0.0), jnp.where(lane < DH_D, 0.0, q)], axis=0)
        qqt = qq.T.astype(BF16)
        qqts.append(qqt)
        if has_prefix:
            st = _dot(kp_ref[:, lanes(e)].astype(BF16), qqt)
            mx = jnp.max(st, axis=0, keepdims=True)
            m_ref[e] = mx
            acc_ref[e] = _dot(with_ones(vpt_ref[lanes(e), :]), jnp.exp2(st - mx).astype(BF16))
        else:
            m_ref[e] = jnp.full(m_ref.shape[1:], -jnp.inf, F32)
            acc_ref[e] = jnp.zeros(acc_ref.shape[1:], F32)

    def step(buf, e, j):
        s_ref, mx_ref = buf
        m_old = m_ref[e]
        m_new = jnp.maximum(m_old, mx_ref[e])
        acc_ref[e] = (jnp.exp2(m_old - m_new) * acc_ref[e]
                      + _dot(vt_ref[e, j], jnp.exp2(s_ref[e] - m_new).astype(BF16)))
        m_ref[e] = m_new

    rel = (lax.broadcasted_iota(jnp.int32, (tk, 2 * tq), 0)
           - lax.broadcasted_iota(jnp.int32, (tk, 2 * tq), 1) % tq)

    def put(buf, e, j, masked):
        s_ref, mx_ref = buf
        st = _dot(kb_ref[e, pl.ds(pl.multiple_of(j * tk, tk), tk), :], qqts[e])
        if masked:
            st = jnp.where(rel <= qi * tq - j * tk, st, -jnp.inf)
        s_ref[e] = st
        mx_ref[e] = jnp.max(st, axis=0, keepdims=True)

    def put_all(buf, j, masked):
        for e in heads:
            put(buf, e, j, masked)

    def step_all(buf, j):
        for e in heads:
            step(buf, e, j)

    assert tq == 2 * tk
    buf_a, buf_b = (sa_ref, ma_ref), (sb_ref, mb_ref)
    put_all(buf_a, 0, True)

    def pair(p, carry):
        j = 2 * p
        put_all(buf_b, j + 1, False)
        step_all(buf_a, j)
        put_all(buf_a, j + 2, False)
        step_all(buf_b, j + 1)
        return carry

    lax.fori_loop(0, qi - 1, pair, 0)

    @pl.when(qi >= 1)
    def _():
        j = 2 * (qi - 1)
        put_all(buf_b, j + 1, False)
        step_all(buf_a, j)
        put_all(buf_a, j + 2, True)
        step_all(buf_b, j + 1)

    half = tq // 2
    cols = (slice(half, tq), slice(tq + half, 2 * tq))
    last = 2 * qi + 1
    tri = (lax.broadcasted_iota(jnp.int32, (tk, tq), 0)
           <= lax.broadcasted_iota(jnp.int32, (tk, tq), 1) % half)
    tails = []
    for e in heads:
        qh = jnp.concatenate([qqts[e][:, c] for c in cols], axis=1)
        st = _dot(kb_ref[e, pl.ds(pl.multiple_of(last * tk, tk), tk), :], qh)
        st = jnp.where(tri, st, -jnp.inf)
        tails.append((st, jnp.max(st, axis=0, keepdims=True)))
    step_all(buf_a, 2 * qi)
    lam = _lam(lq1_ref, lk1_ref, lq2_ref, lk2_ref)
    for e in heads:
        st, mx = tails[e]
        m_old = jnp.concatenate([m_ref[e, :, c] for c in cols], axis=1)
        m_new = jnp.maximum(m_old, mx)
        r = jnp.exp2(m_old - m_new)
        pv = _dot(vt_ref[e, last], jnp.exp2(st - m_new).astype(BF16))
        for i, c in enumerate(cols):
            part = slice(i * half, (i + 1) * half)
            acc_ref[e, :, c] = r[:, part] * acc_ref[e, :, c] + pv[:, part]
        acc = acc_ref[e, 0:DV_D, :] / acc_ref[e, DV_D:DV_D + 1, :]
        ot = acc[:, 0:tq] - lam * acc[:, tq:2 * tq]
        ot = ot * lax.rsqrt(jnp.mean(ot * ot, axis=0, keepdims=True) + EPS) * subln_ref[...]
        o_ref[0, :, lanes(e)] = (ot * (1.0 - LAMBDA_INIT)).T.astype(BF16)


def _attn_prompt(q, k, v, kpre, vpre, lams, subln_col, *, tq, has_prefix):
    vpre_t = vpre.T
    B, T, _ = q.shape
    P = kpre.shape[0]
    tk = tq // 2
    E = ATTN_HEADS
    W = E * LANES
    assert T % tq == 0 and tq % (2 * tk) == 0 and NH_D % E == 0
    lspec = _full((1, DH_D))
    return pl.pallas_call(
        functools.partial(_attn_prompt_body, tq=tq, tk=tk, has_prefix=has_prefix),
        grid=(B, NH_D // E, T // tq),
        in_specs=[
            pl.BlockSpec((1, tq, W), lambda b, h, i: (b, i, h)),
            pl.BlockSpec((1, T, W), lambda b, h, i: (b, 0, h)),
            pl.BlockSpec((1, T, W), lambda b, h, i: (b, 0, h)),
            pl.BlockSpec((P, W), lambda b, h, i: (0, h)),
            pl.BlockSpec((P, W), lambda b, h, i: (0, h)),
            pl.BlockSpec((E * DV_D, P), lambda b, h, i: (h, 0)),
            lspec, lspec, lspec, lspec, _full((DV_D, 1)),
        ],
        out_specs=(pl.BlockSpec((1, tq, W), lambda b, h, i: (b, i, h)),
                   pl.BlockSpec((1, P + T, W), lambda b, h, i: (b, 0, h)),
                   pl.BlockSpec((1, P + T, W), lambda b, h, i: (b, 0, h))),
        out_shape=(jax.ShapeDtypeStruct((B, T, D_MODEL), BF16),
                   jax.ShapeDtypeStruct((B, P + T, D_MODEL), F32),
                   jax.ShapeDtypeStruct((B, P + T, D_MODEL), F32)),
        scratch_shapes=[pltpu.VMEM((E, T, LANES), BF16), pltpu.VMEM((E, T // tk, DV_D + ONES_ROWS, tk), BF16),
                        pltpu.VMEM((E, 1, 2 * tq), F32), pltpu.VMEM((E, DV_D + ONES_ROWS, 2 * tq), F32),
                        pltpu.VMEM((E, tk, 2 * tq), F32), pltpu.VMEM((E, tk, 2 * tq), F32),
                        pltpu.VMEM((E, 1, 2 * tq), F32), pltpu.VMEM((E, 1, 2 * tq), F32)],
        compiler_params=_cparams(("parallel", "parallel", "arbitrary")),
        name="attn_prompt",
    )(q, k, v, kpre, vpre, vpre_t, *lams, subln_col)


QROWS = 8
QSUB = 4
QMAT_ROWS = 2 * NH_D * QSUB
PAGES_PER_STEP = 16


def _attn_sample_body(pt_ref, q_ref, *refs, n_new, n_pg):
    kc_refs, vc_refs = refs[:n_pg], refs[n_pg:2 * n_pg]
    (kn_ref, vn_ref, lq1_ref, lk1_ref, lq2_ref, lk2_ref, subln_ref,
     o_ref, qm_ref, bias_ref, m_ref, l_ref, acc_ref) = refs[2 * n_pg:]
    p = pl.program_id(1)
    R = QMAT_ROWS
    rows = kc_refs[0].shape[1]
    half = R // 2

    def head_of(r):
        return (r % half) // QSUB

    @pl.when(p == 0)
    def _():
        q = q_ref[0].astype(F32)
        sub = lax.broadcasted_iota(jnp.int32, (QROWS, LANES), 0)
        pieces = []
        for j in range(NH_D // 2):
            even = q[:, (2 * j) * LANES:(2 * j + 1) * LANES]
            odd = q[:, (2 * j + 1) * LANES:(2 * j + 2) * LANES]
            pieces.append(jnp.where(sub < QSUB, even, pltpu.roll(odd, QSUB, axis=0)))
        qh = jnp.concatenate(pieces, axis=0)
        lane = lax.broadcasted_iota(jnp.int32, qh.shape, 1)
        qm_ref[...] = jnp.concatenate([jnp.where(lane < DH_D, qh, 0.0), jnp.where(lane < DH_D, 0.0, qh)],
                                      axis=0).astype(BF16)
        row = lax.broadcasted_iota(jnp.int32, (R, rows), 0)
        col = lax.broadcasted_iota(jnp.int32, (R, rows), 1)
        bias_ref[...] = jnp.where(col % NH_D == head_of(row), 0.0, -jnp.inf)
        m_ref[...] = jnp.full(m_ref.shape, -jnp.inf, F32)
        l_ref[...] = jnp.zeros(l_ref.shape, F32)
        acc_ref[...] = jnp.zeros(acc_ref.shape, F32)

    qm = qm_ref[...]
    bias = bias_ref[...]
    s = [_dot_nt(qm, kc[0].astype(BF16)) + bias for kc in kc_refs]
    m_old = m_ref[...]
    m_new = m_old
    for si in s:
        m_new = jnp.maximum(m_new, jnp.max(si, axis=1, keepdims=True))
    r = jnp.exp2(m_old - m_new)
    l_new = r * l_ref[...]
    acc = r * acc_ref[...]
    for si, vc in zip(s, vc_refs):
        pr = jnp.exp2(si - m_new)
        l_new = l_new + jnp.sum(pr, axis=1, keepdims=True)
        acc = acc + _dot(pr.astype(BF16), vc[0].astype(BF16))
    m_ref[...] = m_new
    l_ref[...] = l_new
    acc_ref[...] = acc

    @pl.when(p == pl.num_programs(1) - 1)
    def _():
        zpad = jnp.zeros((LANES - kn_ref.shape[1], LANES), F32)
        kn = jnp.concatenate([kn_ref[0], zpad], axis=0).astype(BF16)
        vn = jnp.concatenate([vn_ref[0], zpad], axis=0).astype(BF16)
        sn = _dot_nt(qm, kn)
        row = lax.broadcasted_iota(jnp.int32, sn.shape, 0)
        col = lax.broadcasted_iota(jnp.int32, sn.shape, 1)
        tok = col // NH_D
        ok = (col % NH_D == head_of(row)) & (tok <= row % QSUB) & (tok < n_new)
        _online_softmax_step(m_ref, l_ref, acc_ref, jnp.where(ok, sn, -jnp.inf), vn)
        lam = _lam(lq1_ref, lk1_ref, lq2_ref, lk2_ref)
        o = acc_ref[0:half, :] / l_ref[0:half, :] - lam * (acc_ref[half:R, :] / l_ref[half:R, :])
        o_ref[0] = (_rms(o, subln_ref[...]) * (1.0 - LAMBDA_INIT)).astype(BF16)


def _attn_sample(page_table, q, cache_k, cache_v, knew, vnew, lams, subln, *, n_new):
    B, n_pages = page_table.shape
    rows = cache_k.shape[1]
    R = QMAT_ROWS
    n_pg = math.gcd(n_pages, PAGES_PER_STEP)
    lspec = pl.BlockSpec((1, DH_D), lambda b, p, pt: (0, 0))
    pgs = [pl.BlockSpec((1, rows, LANES), lambda b, p, pt, i=i: (pt[b, p * n_pg + i], 0, 0)) for i in range(n_pg)]
    new = pl.BlockSpec((1, n_new * NH_D, LANES), lambda b, p, pt: (b, 0, 0))
    return pl.pallas_call(
        functools.partial(_attn_sample_body, n_new=n_new, n_pg=n_pg),
        grid_spec=pltpu.PrefetchScalarGridSpec(
            num_scalar_prefetch=1,
            grid=(B, n_pages // n_pg),
            in_specs=[pl.BlockSpec((1, QROWS, D_MODEL), lambda b, p, pt: (b, 0, 0))] + pgs + pgs
                     + [new, new, lspec, lspec, lspec, lspec, pl.BlockSpec((1, DV_D), lambda b, p, pt: (0, 0))],
            out_specs=pl.BlockSpec((1, R // 2, DV_D), lambda b, p, pt: (b, 0, 0)),
            scratch_shapes=[pltpu.VMEM((R, LANES), BF16), pltpu.VMEM((R, rows), F32), pltpu.VMEM((R, 1), F32),
                            pltpu.VMEM((R, 1), F32), pltpu.VMEM((R, DV_D), F32)],
        ),
        out_shape=jax.ShapeDtypeStruct((B, R // 2, DV_D), BF16),
        compiler_params=_cparams(("parallel", "arbitrary")),
        name="attn_sample",
    )(page_table, q, *([cache_k] * n_pg), *([cache_v] * n_pg), knew, vnew, *lams, subln)


def _merge_body(hm_ref, o_ref, ga_ref, gb_ref, x_ref, wa_ref, wb_ref, wo_ref, y_ref):
    ya = _dot(hm_ref[...], wa_ref[...])
    yb = _dot(o_ref[...], wb_ref[...])
    merged = _sigmoid(ga_ref[...]) * ya + _sigmoid(gb_ref[...]) * yb
    y_ref[...] = x_ref[...] + _dot(merged.astype(BF16), wo_ref[...])


def _merge(hm, o, ga, gb, x, wa, wb, wo, *, tm=1024):
    n = hm.shape[0]
    tm = _row_tile(n, tm)
    row = pl.BlockSpec((tm, D_MODEL), lambda i: (i, 0))
    wspec = _resident((D_MODEL, D_MODEL))
    return pl.pallas_call(
        _merge_body,
        grid=(n // tm,),
        in_specs=[row, row, row, row, row, wspec, wspec, wspec],
        out_specs=row,
        out_shape=jax.ShapeDtypeStruct((n, D_MODEL), F32),
        compiler_params=_cparams(("parallel",)),
        name="merge",
    )(hm, o, ga, gb, x, wa, wb, wo)


def _rope_tables(pos):
    half = ROT_DIM // 2
    inv_freq = ROPE_THETA ** (-jnp.arange(half, dtype=F32) / half)
    d = jnp.arange(LANES, dtype=jnp.int32) % DH_D
    freq = jnp.where(d < ROT_DIM, inv_freq[d % half], 0.0)
    ang = pos.astype(F32)[:, None] * freq[None, :]
    c, s = jnp.cos(ang), jnp.sin(ang)
    s_up = jnp.where(d < half, -s, 0.0)
    s_dn = jnp.where(d >= half, s, 0.0)
    return c, s_up, s_dn


def _pad_rows(a, rows, axis, front=False):
    pad = [(0, 0)] * a.ndim
    pad[axis] = (rows - a.shape[axis], 0) if front else (0, rows - a.shape[axis])
    return jnp.pad(a, pad)


def kernel(x_prompt, x_sample, cache_k, cache_v, page_table, state_C, state_n, state_m, state_conv, meta_tokens, ffn1_norm, ffn1_w_gate, ffn1_w_up, ffn1_w_down, mix_norm, w_in, conv_w, conv_b, w_mq, w_mk, w_mv, b_igate, b_fgate, mlstm_norm, lambda_q1, lambda_k1, lambda_q2, lambda_k2, subln, w_proj_a, w_proj_b, w_out, ffn2_norm, ffn2_w_gate, ffn2_w_up, ffn2_w_down, final_norm):
    bp, seq, _ = x_prompt.shape
    bs, ts, _ = x_sample.shape
    n_pages = page_table.shape[1]
    page = cache_k.shape[2]
    past_len = n_pages * page
    assert w_in.shape[0] == 1 and ts <= QSUB

    win = w_in[0]
    offs = [0]
    for width in (D_MODEL, D_MODEL, NH_M, NH_M, D_MODEL, D_MODEL, D_MODEL, D_MODEL, D_MODEL):
        offs.append(offs[-1] + width)
    col = lambda i: win[:, offs[i]:offs[i + 1]]
    wgate = jnp.pad(jnp.concatenate([col(2), col(3)], axis=1), ((0, 0), (0, LANES - 2 * NH_M)))
    wproj = {"xm": col(0).astype(BF16), "om": col(1).astype(BF16), "q": col(4).astype(BF16),
             "k": col(5).astype(BF16), "v": col(6).astype(BF16), "ga": col(7).astype(BF16),
             "gb": col(8).astype(BF16), "gate": wgate.astype(BF16)}
    gbias = jnp.pad(jnp.concatenate([b_igate[0], b_fgate[0]])[None, :], ((0, 0), (0, LANES - 2 * NH_M)))
    wm = {"conv_w": conv_w[0], "conv_b": conv_b[0][None, :],
          "mq": w_mq[0].astype(BF16), "mk": (w_mk[0] * DH_M ** -0.5).astype(BF16), "mv": w_mv[0].astype(BF16),
          "mvt": jnp.swapaxes(w_mv[0], 1, 2).astype(BF16), "mnorm": mlstm_norm[0].reshape(1, D_MODEL)}
    lams = (lambda_q1, lambda_k1, lambda_q2, lambda_k2)
    row = lambda a: a.reshape(1, -1)
    f1 = (row(ffn1_norm[0]), ffn1_w_gate[0].astype(BF16), ffn1_w_up[0].astype(BF16), ffn1_w_down[0].astype(BF16))
    f2 = (row(ffn2_norm[0]), ffn2_w_gate[0].astype(BF16), ffn2_w_up[0].astype(BF16), ffn2_w_down[0].astype(BF16))
    fin = row(final_norm)
    wa, wb, wo = w_proj_a[0].astype(BF16), w_proj_b[0].astype(BF16), w_out[0].astype(BF16)

    n_main = bp * seq
    n_samp = bs * ts
    x_main = x_prompt.reshape(n_main, D_MODEL)
    x_small = jnp.concatenate([x_sample.reshape(n_samp, D_MODEL), meta_tokens], axis=0)

    tm_in = min(seq, 512)
    tabs_main = _rope_tables(N_META + jnp.arange(seq, dtype=jnp.int32))
    pos_small = jnp.concatenate([jnp.tile(past_len + jnp.arange(ts, dtype=jnp.int32), bs),
                                 jnp.arange(N_META, dtype=jnp.int32)])
    tabs_small = _rope_tables(pos_small)

    x1_main = _ffn(x_main, *f1, fin, final_norm=False)
    x1_small = _ffn(x_small, *f1, fin, final_norm=False)
    pm = _inproj(x1_main, row(mix_norm[0]), wproj, gbias, tabs_main, n_rope_blocks=seq // tm_in, tm=tm_in)
    ps = _inproj(x1_small, row(mix_norm[0]), wproj, gbias, tabs_small, n_rope_blocks=1, tm=x_small.shape[0])
    xm_m, om_m, q_m, k_m, v_m, ga_m, gb_m, gcol_m, grow_m = pm
    xm_s, om_s, q_s, k_s, v_s, ga_s, gb_s, gcol_s, grow_s = ps

    def split(a):
        return a[:n_samp], a[n_samp:]

    xm_smp, xm_meta = split(xm_s)
    om_smp, om_meta = split(om_s)
    gcol_smp, gcol_meta = split(gcol_s)
    grow_smp, grow_meta = grow_s[:, :n_samp], grow_s[:, n_samp:]

    zc = jnp.zeros((1, NH_M, DH_M, DH_M), F32)
    zn = jnp.zeros((1, SUBLANES, DH_M), F32)
    zm = jnp.zeros((1, SUBLANES, LANES), F32)
    zbuf = jnp.zeros((1, SUBLANES, D_MODEL), F32)
    hm_meta, c_meta, n_meta, m_meta = _mlstm(
        xm_meta[None], om_meta[None], gcol_meta[None], grow_meta.reshape(1, 1, SUBLANES, N_META),
        zbuf, zc, zn, zm, wm, chunk=N_META, nb=1)

    chunk = _row_tile(seq, MLSTM_CHUNK)
    grow_main = grow_m.reshape(SUBLANES, bp, seq // chunk, chunk).transpose(1, 2, 0, 3)
    buf_main = _pad_rows(xm_meta[N_META - (CONV_W - 1):], SUBLANES, 0, front=True)[None]
    hm_main, c_main, n_main_st, m_main = _mlstm(
        xm_m.reshape(bp, seq, D_MODEL), om_m.reshape(bp, seq, D_MODEL), gcol_m.reshape(bp, seq, LANES),
        grow_main, buf_main, c_meta, n_meta, m_meta, wm, chunk=chunk, nb=_group(bp))

    pad_t = lambda a: _pad_rows(a.reshape(bs, ts, a.shape[-1]), QROWS, 1)
    gcol_pad = jnp.concatenate(
        [gcol_smp.reshape(bs, ts, LANES),
         jnp.broadcast_to(jnp.where(jnp.arange(LANES) < NH_M, NEG_BIG, 0.0).astype(F32), (bs, QROWS - ts, LANES))],
        axis=1)
    grow_pad = jnp.concatenate(
        [grow_smp.reshape(SUBLANES, bs, ts),
         jnp.broadcast_to(jnp.where(jnp.arange(SUBLANES) < NH_M, NEG_BIG, 0.0).astype(F32)[:, None, None],
                          (SUBLANES, bs, QROWS - ts))], axis=2).transpose(1, 0, 2)[:, None]
    hm_smp, c_smp, n_smp, m_smp = _mlstm(
        pad_t(xm_smp), pad_t(om_smp), gcol_pad, grow_pad,
        _pad_rows(state_conv[0], SUBLANES, 1, front=True), state_C[0],
        _pad_rows(state_n[0], SUBLANES, 1), jnp.broadcast_to(_pad_rows(state_m[0], SUBLANES, 1)[:, :, None],
                                                             (bs, SUBLANES, LANES)),
        wm, chunk=QROWS, nb=_group(bs))

    q_smp, q_meta = split(q_s)
    k_smp, k_meta = split(k_s)
    v_smp, v_meta = split(v_s)
    sub = row(subln[0])
    sub_col = subln[0].reshape(DV_D, 1)
    o_meta = _attn_prompt(q_meta[None], k_meta[None], v_meta[None], k_meta, v_meta, lams, sub_col,
                          tq=N_META, has_prefix=False)[0]
    tq = min(seq, ATTN_TQ)
    o_main, k_full, v_full = _attn_prompt(q_m.reshape(bp, seq, D_MODEL), k_m.reshape(bp, seq, D_MODEL),
                                          v_m.reshape(bp, seq, D_MODEL), k_meta, v_meta, lams, sub_col,
                                          tq=tq, has_prefix=True)
    del o_meta
    k_new = k_smp.reshape(1, bs, ts, NH_D, DV_D)
    v_new = v_smp.reshape(1, bs, ts, NH_D, DV_D)
    o_smp = _attn_sample(page_table, pad_t(q_smp), cache_k.reshape(-1, page * NH_D, DV_D),
                         cache_v.reshape(-1, page * NH_D, DV_D), k_new.reshape(bs, ts * NH_D, DV_D),
                         v_new.reshape(bs, ts * NH_D, DV_D), lams, sub, n_new=ts)
    o_smp = o_smp.reshape(bs, NH_D, QSUB, DV_D)[:, :, :ts].transpose(0, 2, 1, 3).reshape(n_samp, D_MODEL)

    x2_main = _merge(hm_main.reshape(n_main, D_MODEL), o_main.reshape(n_main, D_MODEL), ga_m, gb_m, x1_main,
                     wa, wb, wo)
    y_main = _ffn(x2_main, *f2, fin, final_norm=True)
    x2_smp = _merge(hm_smp[:, :ts].reshape(n_samp, D_MODEL), o_smp,
                    ga_s, gb_s, x1_small, wa, wb, wo)
    y_smp = _ffn(x2_smp, *f2, fin, final_norm=True)
    del hm_meta

    xm_main3 = xm_m.reshape(bp, seq, D_MODEL)
    return (
        y_main.reshape(bp, seq, D_MODEL),
        y_smp.reshape(bs, ts, D_MODEL),
        k_full.reshape(1, bp, seq + N_META, NH_D, DV_D),
        v_full.reshape(1, bp, seq + N_META, NH_D, DV_D),
        c_main[None],
        n_main_st[None, :, :NH_M, :],
        m_main[None, :, :NH_M, 0],
        xm_main3[None, :, seq - (CONV_W - 1):, :],
        k_new,
        v_new,
        c_smp[None],
        n_smp[None, :, :NH_M, :],
        m_smp[None, :, :NH_M, 0],
        xm_smp.reshape(bs, ts, D_MODEL)[None, :, ts - (CONV_W - 1):, :],
    )
```
